```python
import jax, jax.numpy as jnp
from jax import lax
import numpy as np

D_MODEL = 1024
BATCH = 16
SEQ = 2048
DEPTH = 4

CHUNK = 64
D_MIX = D_MODEL
SSD_WIDTH = D_MIX // 2
ATTN_WIDTH = D_MIX - SSD_WIDTH
SSD_HEAD_DIM = 64
SSD_HEADS = SSD_WIDTH // SSD_HEAD_DIM
SSD_GROUPS = 2
SSD_HPG = SSD_HEADS // SSD_GROUPS
SSD_STATE = 128
SSD_CONV = 4
SSD_CHUNK = CHUNK
XBC_DIM = SSD_WIDTH + 2 * SSD_GROUPS * SSD_STATE
ATTN_HEAD_DIM = 64
ATTN_HEADS = ATTN_WIDTH // ATTN_HEAD_DIM
LEFT_CHUNKS = 8
BAND = (LEFT_CHUNKS + 1) * CHUNK
MAX_REL = 128
N_REL = 2 * MAX_REL + 1
FFN_DIM = 2816
FFN_CONV = 3
ADA_DIM = 6 * D_MODEL
IN_SIZES = (SSD_WIDTH, XBC_DIM, SSD_HEADS, ATTN_WIDTH, ATTN_WIDTH, ATTN_WIDTH)
IN_DIM = sum(IN_SIZES)
IN_SPLITS = [int(v) for v in np.cumsum(IN_SIZES)[:-1]]
EPS = 1e-6

kernel_name = "hymba_ssd_chunkattn_convffn_adaln"


def rmsnorm(x, w):
    xf = x.astype(jnp.float32)
    y = xf * lax.rsqrt(jnp.mean(xf * xf, axis=-1, keepdims=True) + EPS)
    return (y * w.astype(jnp.float32)).astype(x.dtype)


def causal_dwconv(x, w):
    k, ch = w.shape
    return lax.conv_general_dilated(x, w[:, None, :].astype(x.dtype), window_strides=(1,),
                                    padding=[(k - 1, 0)], dimension_numbers=('NWC', 'WIO', 'NWC'),
                                    feature_group_count=ch)


def ssd_scan(xdt, adt, bm, cm):
    b, s = xdt.shape[:2]
    nc = s // SSD_CHUNK
    L = SSD_CHUNK
    X = xdt.reshape(b, nc, L, SSD_GROUPS, SSD_HPG, SSD_HEAD_DIM)
    A = adt.reshape(b, nc, L, SSD_GROUPS, SSD_HPG)
    Bm = bm.reshape(b, nc, L, SSD_GROUPS, SSD_STATE)
    Cm = cm.reshape(b, nc, L, SSD_GROUPS, SSD_STATE)
    a_cs = jnp.cumsum(A, axis=2)
    seg = a_cs[:, :, :, None] - a_cs[:, :, None]
    causal = jnp.tril(jnp.ones((L, L), dtype=bool))[:, :, None, None]
    decay_ls = jnp.exp(jnp.where(causal, seg, -jnp.inf))
    cb = jnp.einsum('bclgn,bcsgn->bclsg', Cm, Bm)
    y_diag = jnp.einsum('bclsg,bclsge,bcsgep->bclgep', cb, decay_ls, X)
    decay_to_end = jnp.exp(a_cs[:, :, -1:] - a_cs)
    states = jnp.einsum('bclgn,bclge,bclgep->bcgepn', Bm, decay_to_end, X)
    chunk_decay = jnp.exp(a_cs[:, :, -1])

    def step(h, inp):
        st, dec = inp
        return h * dec[..., None, None] + st, h

    _, prev = lax.scan(step, jnp.zeros_like(states[:, 0]),
                       (jnp.moveaxis(states, 1, 0), jnp.moveaxis(chunk_decay, 1, 0)))
    prev = jnp.moveaxis(prev, 0, 1)
    y_off = jnp.einsum('bclgn,bcgepn,bclge->bclgep', Cm, prev, jnp.exp(a_cs))
    return (y_diag + y_off).reshape(b, s, SSD_HEADS, SSD_HEAD_DIM)


def ssd_mixer(z, xbc, dt_raw, conv_w, conv_b, dt_bias, a_log, d_skip, norm_w):
    b, s, _ = xbc.shape
    xbc = jax.nn.silu(causal_dwconv(xbc, conv_w) + conv_b.astype(xbc.dtype))
    xs, bm, cm = jnp.split(xbc.astype(jnp.float32), [SSD_WIDTH, SSD_WIDTH + SSD_GROUPS * SSD_STATE], axis=-1)
    xs = xs.reshape(b, s, SSD_HEADS, SSD_HEAD_DIM)
    bm = bm.reshape(b, s, SSD_GROUPS, SSD_STATE)
    cm = cm.reshape(b, s, SSD_GROUPS, SSD_STATE)
    dt = jax.nn.softplus(dt_raw.astype(jnp.float32) + dt_bias.astype(jnp.float32))
    A = -jnp.exp(a_log.astype(jnp.float32))
    y = ssd_scan(xs * dt[..., None], dt * A, bm, cm)
    y = y + d_skip.astype(jnp.float32)[:, None] * xs
    y = y.reshape(b, s, SSD_WIDTH) * jax.nn.silu(z.astype(jnp.float32))
    return rmsnorm(y, norm_w).astype(z.dtype)


def chunk_attention(q, k, v, rel_bias):
    b, s, _ = q.shape
    nc = s // CHUNK
    pad = LEFT_CHUNKS * CHUNK
    qc = q.reshape(b, nc, CHUNK, ATTN_HEADS, ATTN_HEAD_DIM)

    def band(t):
        tp = jnp.pad(t.reshape(b, s, ATTN_HEADS, ATTN_HEAD_DIM), ((0, 0), (pad, 0), (0, 0), (0, 0)))
        tp = tp.reshape(b, nc + LEFT_CHUNKS, CHUNK, ATTN_HEADS, ATTN_HEAD_DIM)
        return jnp.concatenate([tp[:, w:w + nc] for w in range(LEFT_CHUNKS + 1)], axis=2)

    kb, vb = band(k), band(v)
    scores = jnp.einsum('bcqhd,bckhd->bhcqk', qc, kb).astype(jnp.float32) * (ATTN_HEAD_DIM ** -0.5)
    qi = jnp.arange(CHUNK)[:, None]
    kj = jnp.arange(BAND)[None, :]
    rel_idx = jnp.clip(qi - kj + pad, -MAX_REL, MAX_REL) + MAX_REL
    bias = rel_bias.astype(jnp.float32)[:, rel_idx]
    key_pos = jnp.arange(nc)[:, None] * CHUNK - pad + kj
    valid = (key_pos >= 0)[:, None, :]
    scores = jnp.where(valid, scores + bias[:, None], jnp.finfo(jnp.float32).min)
    p = jax.nn.softmax(scores, axis=-1).astype(v.dtype)
    o = jnp.einsum('bhcqk,bckhd->bcqhd', p, vb)
    return o.reshape(b, s, ATTN_WIDTH)


def conv_ffn(h, w_up, conv_w, w_down):
    u = causal_dwconv(h @ w_up, conv_w)
    g, val = jnp.split(u, 2, axis=-1)
    return (jax.nn.silu(g) * val) @ w_down


def setup_inputs(seed: int = 0) -> dict:
    key = jax.random.key(seed)
    ks = jax.random.split(key, 24)
    f32 = jnp.float32
    nrm = lambda k, shape, sc: jax.random.normal(k, shape, f32) * sc
    u = jax.random.uniform(ks[10], (DEPTH, SSD_HEADS), f32)
    dt0 = jnp.exp(u * (jnp.log(0.1) - jnp.log(0.001)) + jnp.log(0.001))
    return {
        "x": nrm(ks[0], (BATCH, SEQ, D_MODEL), 1.0),
        "c": nrm(ks[1], (BATCH, D_MODEL), 1.0),
        "norm_mix_w": 1.0 + nrm(ks[2], (DEPTH, D_MODEL), 0.05),
        "w_ada": nrm(ks[3], (DEPTH, D_MODEL, ADA_DIM), 0.02),
        "b_ada": nrm(ks[4], (DEPTH, ADA_DIM), 0.02),
        "w_in": nrm(ks[5], (DEPTH, D_MODEL, IN_DIM), D_MODEL ** -0.5),
        "ssd_conv_w": nrm(ks[6], (DEPTH, SSD_CONV, XBC_DIM), SSD_CONV ** -0.5),
        "ssd_conv_b": nrm(ks[7], (DEPTH, XBC_DIM), 0.02),
        "dt_bias": dt0 + jnp.log(-jnp.expm1(-dt0)),
        "a_log": jnp.log(jax.random.uniform(ks[8], (DEPTH, SSD_HEADS), f32, 1.0, 16.0)),
        "d_skip": 1.0 + nrm(ks[9], (DEPTH, SSD_HEADS), 0.1),
        "ssd_norm_w": 1.0 + nrm(ks[11], (DEPTH, SSD_WIDTH), 0.05),
        "rel_bias": nrm(ks[12], (DEPTH, ATTN_HEADS, N_REL), 0.5),
        "w_out": nrm(ks[13], (DEPTH, D_MIX, D_MODEL), D_MIX ** -0.5),
        "norm_ffn_w": 1.0 + nrm(ks[14], (DEPTH, D_MODEL), 0.05),
        "w_up": nrm(ks[15], (DEPTH, D_MODEL, 2 * FFN_DIM), D_MODEL ** -0.5),
        "ffn_conv_w": nrm(ks[16], (DEPTH, FFN_CONV, 2 * FFN_DIM), FFN_CONV ** -0.5),
        "w_down": nrm(ks[17], (DEPTH, FFN_DIM, D_MODEL), FFN_DIM ** -0.5),
        "final_norm_w": 1.0 + nrm(ks[18], (D_MODEL,), 0.05),
    }


def reference(x, c, norm_mix_w, w_ada, b_ada, w_in, ssd_conv_w, ssd_conv_b, dt_bias, a_log, d_skip,
              ssd_norm_w, rel_bias, w_out, norm_ffn_w, w_up, ffn_conv_w, w_down, final_norm_w):
    c_act = jax.nn.silu(c)
    for l in range(DEPTH):
        mod = (c_act @ w_ada[l] + b_ada[l])[:, None, :]
        sh_m, sc_m, g_m, sh_f, sc_f, g_f = jnp.split(mod, 6, axis=-1)
        h = rmsnorm(x, norm_mix_w[l]) * (1 + sc_m) + sh_m
        z, xbc, dt_raw, q, k, v = jnp.split(h @ w_in[l], IN_SPLITS, axis=-1)
        y_ssd = ssd_mixer(z, xbc, dt_raw, ssd_conv_w[l], ssd_conv_b[l], dt_bias[l], a_log[l],
                          d_skip[l], ssd_norm_w[l])
        y_att = chunk_attention(q, k, v, rel_bias[l])
        x = x + g_m * (jnp.concatenate([y_ssd, y_att], axis=-1) @ w_out[l])
        h = rmsnorm(x, norm_ffn_w[l]) * (1 + sc_f) + sh_f
        x = x + g_f * conv_ffn(h, w_up[l], ffn_conv_w[l], w_down[l])
    return rmsnorm(x, final_norm_w)
```

```python
import functools

import numpy as np
import jax
import jax.numpy as jnp
from jax import lax
from jax.experimental import pallas as pl
from jax.experimental.pallas import tpu as pltpu

F32 = jnp.float32
BF16 = jnp.bfloat16

D_MODEL = 1024
CHUNK = 64
SSD_WIDTH = 512
SSD_HEAD_DIM = 64
SSD_HEADS = 8
SSD_GROUPS = 2
SSD_STATE = 128
SSD_CONV = 4
XBC_DIM = SSD_WIDTH + 2 * SSD_GROUPS * SSD_STATE
GROUP_WIDTH = SSD_WIDTH // SSD_GROUPS
ATTN_WIDTH = 512
ATTN_HEAD_DIM = 64
ATTN_HEADS = 8
LEFT_CHUNKS = 8
BAND = (LEFT_CHUNKS + 1) * CHUNK
MAX_REL = 128
FFN_DIM = 2816
FFN_CONV = 3
EPS = 1e-6

HEAD_SHIFT = 6
LANES = 128
ROW_TILE = 512
SEQ_TILE = 512
FFN_CHUNK = 256
HALO = 16
VMEM_LIMIT = 56 * 1024 * 1024

NT_DIMS = (((1,), (1,)), ((), ()))
TN_DIMS = (((0,), (0,)), ((), ()))


def _dot(a, b):
    return jnp.dot(a, b, preferred_element_type=F32)


def _dot_exact(a, b):
    return jnp.dot(a, b, preferred_element_type=F32, precision=lax.Precision.HIGHEST)


def _silu(x):
    return x / (1.0 + jnp.exp(-x))


def _softplus(x):
    return jnp.maximum(x, 0.0) + jnp.log1p(jnp.exp(-jnp.abs(x)))


def _rms_scale(x):
    return x * lax.rsqrt(jnp.mean(x * x, axis=-1, keepdims=True) + EPS)


def _params(*semantics):
    return pltpu.CompilerParams(dimension_semantics=semantics, vmem_limit_bytes=VMEM_LIMIT)


def _resident(shape):
    return pl.BlockSpec(shape, lambda *_: (0,) * len(shape), pipeline_mode=pl.Buffered(1))


def _ada_kernel(c_ref, w_ref, b_ref, o_ref):
    c = c_ref[...]
    c_act = _silu(c).astype(BF16)
    o_ref[0] = _dot(c_act, w_ref[0].astype(BF16)) + b_ref[0]


def _ada_call(c, w_ada, b_ada):
    depth, d, n = w_ada.shape
    b = c.shape[0]
    tn = n // 4
    return pl.pallas_call(
        _ada_kernel,
        grid=(depth, n // tn),
        in_specs=[
            pl.BlockSpec((b, d), lambda l, j: (0, 0)),
            pl.BlockSpec((1, d, tn), lambda l, j: (l, 0, j)),
            pl.BlockSpec((1, 1, tn), lambda l, j: (l, 0, j)),
        ],
        out_specs=pl.BlockSpec((1, b, tn), lambda l, j: (l, 0, j)),
        out_shape=jax.ShapeDtypeStruct((depth, b, n), F32),
        compiler_params=_params("arbitrary", "arbitrary"),
        name="ada_mod",
    )(c, w_ada, b_ada.reshape(depth, 1, n))


def _inproj_kernel(x_ref, mod_ref, nw_ref, wz_ref, wxbc_ref, wdt_ref, wqkv_ref,
                   z_ref, xbc_ref, dt_ref, q_ref, k_ref, v_ref):
    h = _rms_scale(x_ref[...]) * nw_ref[...]
    h = h * (1.0 + mod_ref[0, 1:2, :]) + mod_ref[0, 0:1, :]
    hb = h.astype(BF16)
    z_ref[...] = _dot(hb, wz_ref[...])
    xbc_ref[...] = _dot(hb, wxbc_ref[...])
    dt_ref[...] = _dot(hb, wdt_ref[...])
    qkv = _dot(hb, wqkv_ref[...])
    q_ref[...] = (qkv[:, :ATTN_WIDTH] * (ATTN_HEAD_DIM ** -0.5)).astype(BF16)
    k_ref[...] = qkv[:, ATTN_WIDTH:2 * ATTN_WIDTH].astype(BF16)
    v_ref[...] = qkv[:, 2 * ATTN_WIDTH:].astype(BF16)


def _inproj_call(x, mod, nw, wz, wxbc, wdt, wqkv, seq):
    t, d = x.shape
    tm = ROW_TILE
    tiles_per_seq = seq // tm
    row = lambda w: pl.BlockSpec((tm, w), lambda i: (i, 0))
    outs = [(SSD_WIDTH, F32), (XBC_DIM, F32), (SSD_WIDTH, F32),
            (ATTN_WIDTH, BF16), (ATTN_WIDTH, BF16), (ATTN_WIDTH, BF16)]
    return pl.pallas_call(
        _inproj_kernel,
        grid=(t // tm,),
        in_specs=[
            row(d),
            pl.BlockSpec((1, 6, d), lambda i: (i // tiles_per_seq, 0, 0)),
            _resident((1, d)),
            _resident(wz.shape), _resident(wxbc.shape), _resident(wdt.shape), _resident(wqkv.shape),
        ],
        out_specs=[row(w) for w, _ in outs],
        out_shape=[jax.ShapeDtypeStruct((t, w), dt) for w, dt in outs],
        compiler_params=_params("arbitrary"),
        name="in_proj",
    )(x, mod, nw, wz, wxbc, wdt, wqkv)


def _ssd_kernel(xbc_ref, halo_ref, dt_ref, z_ref, cw_ref, cb_ref, dtb_ref, alog_ref, dskip_ref, nw_ref,
                y_ref, conv_scr, xc_scr, state_scr):
    ts = xbc_ref.shape[0]
    i = pl.program_id(1)

    @pl.when(i == 0)
    def _():
        state_scr[...] = jnp.zeros_like(state_scr)

    conv_scr[0:8, :] = jnp.where(i > 0, halo_ref[...], 0.0)
    conv_scr[8:8 + ts, :] = xbc_ref[...]
    acc = cb_ref[...] + conv_scr[8:8 + ts, :] * cw_ref[3:4, :]
    for k in range(SSD_CONV - 1):
        off = 8 - (SSD_CONV - 1) + k
        acc = acc + conv_scr[off:off + ts, :] * cw_ref[k:k + 1, :]
    xc_scr[...] = _silu(acc)

    a_neg = -jnp.exp(alog_ref[...])
    dskip = dskip_ref[...]
    nw = nw_ref[...]
    dtb = dtb_ref[...]

    rows = lax.broadcasted_iota(jnp.int32, (CHUNK, CHUNK), 0)
    cols = lax.broadcasted_iota(jnp.int32, (CHUNK, CHUNK), 1)
    tri = (rows >= cols).astype(F32)
    row_w = lax.broadcasted_iota(jnp.int32, (CHUNK, SSD_WIDTH), 0)
    lane_w = lax.broadcasted_iota(jnp.int32, (CHUNK, SSD_WIDTH), 1)
    pos_w = lane_w & (SSD_HEAD_DIM - 1)
    diag_w = row_w == pos_w
    causal_w = row_w >= pos_w
    lane_g = lax.broadcasted_iota(jnp.int32, (CHUNK, GROUP_WIDTH), 1)
    heads_per_group = SSD_HEADS // SSD_GROUPS

    def chunk_body(c, carry):
        r0 = pl.multiple_of(c * CHUNK, CHUNK)
        xs = xc_scr[pl.ds(r0, CHUNK), 0:SSD_WIDTH]
        dt = _softplus(dt_ref[pl.ds(r0, CHUNK), :] + dtb)
        adt = dt * a_neg
        xdt = xs * dt
        a_cs = _dot_exact(tri, adt)
        a_last = a_cs[CHUNK - 1:CHUNK, :]
        a_row = jnp.sum(jnp.where(diag_w, a_cs, 0.0), axis=0, keepdims=True)
        decay = jnp.exp(jnp.where(causal_w, a_cs - a_row, -jnp.inf))
        decay_in = jnp.exp(a_cs)
        decay_out = jnp.exp(a_last - a_cs)
        chunk_decay = jnp.exp(a_last)

        ys = []
        for g in range(SSD_GROUPS):
            gl = slice(g * GROUP_WIDTH, (g + 1) * GROUP_WIDTH)
            b_g = xc_scr[pl.ds(r0, CHUNK), SSD_WIDTH + g * SSD_STATE:SSD_WIDTH + (g + 1) * SSD_STATE]
            c_g = xc_scr[pl.ds(r0, CHUNK),
                         SSD_WIDTH + (SSD_GROUPS + g) * SSD_STATE:SSD_WIDTH + (SSD_GROUPS + g + 1) * SSD_STATE]
            b_gb = b_g.astype(BF16)
            c_gb = c_g.astype(BF16)
            cb_rep = lax.dot_general(c_gb, jnp.concatenate([b_gb] * heads_per_group, axis=0), NT_DIMS,
                                     preferred_element_type=F32)
            m = (cb_rep * decay[:, gl]).astype(BF16)
            x_gb = xdt[:, gl].astype(BF16)
            x_blockdiag = jnp.concatenate(
                [jnp.where(lane_g >> HEAD_SHIFT == j, x_gb, jnp.zeros_like(x_gb)) for j in range(heads_per_group)],
                axis=0)
            y_diag = _dot(m, x_blockdiag)
            prev = state_scr[:, gl]
            y_off = _dot(c_gb, prev.astype(BF16)) * decay_in[:, gl]
            xd = (xdt[:, gl] * decay_out[:, gl]).astype(BF16)
            new_states = lax.dot_general(b_gb, xd, TN_DIMS, preferred_element_type=F32)
            state_scr[:, gl] = prev * chunk_decay[:, gl] + new_states
            ys.append(y_diag + y_off)
        y = jnp.concatenate(ys, axis=1) + dskip * xs
        y = y * _silu(z_ref[pl.ds(r0, CHUNK), :])
        y_ref[pl.ds(r0, CHUNK), :] = (_rms_scale(y) * nw).astype(BF16)
        return carry

    lax.fori_loop(0, ts // CHUNK, chunk_body, 0)


def _ssd_call(xbc, dt, z, cw, cb, dtb, alog, dskip, nw, batch, seq):
    t = xbc.shape[0]
    ts = SEQ_TILE
    nt = seq // ts
    tile = lambda w: pl.BlockSpec((ts, w), lambda b, i: (b * nt + i, 0))
    halo_blocks = ts // 8
    return pl.pallas_call(
        _ssd_kernel,
        grid=(batch, nt),
        in_specs=[
            tile(XBC_DIM),
            pl.BlockSpec((8, XBC_DIM), lambda b, i: (jnp.maximum((b * nt + i) * halo_blocks - 1, 0), 0)),
            tile(SSD_WIDTH), tile(SSD_WIDTH),
            _resident(cw.shape), _resident(cb.shape), _resident(dtb.shape), _resident(alog.shape),
            _resident(dskip.shape), _resident(nw.shape),
        ],
        out_specs=tile(SSD_WIDTH),
        out_shape=jax.ShapeDtypeStruct((t, SSD_WIDTH), BF16),
        scratch_shapes=[
            pltpu.VMEM((ts + 8, XBC_DIM), F32),
            pltpu.VMEM((ts, XBC_DIM), F32),
            pltpu.VMEM((SSD_STATE, SSD_WIDTH), F32),
        ],
        compiler_params=_params("arbitrary", "arbitrary"),
        name="ssd_mixer",
    )(xbc, xbc, dt, z, cw, cb, dtb, alog, dskip, nw)


def _attn_kernel(q_ref, kp_ref, kc_ref, vp_ref, vc_ref, bias_ref, o_ref, k_scr, v_scr):
    ts = q_ref.shape[0]
    hist = LEFT_CHUNKS * CHUNK
    i = pl.program_id(1)
    k_scr[0:hist, :] = kp_ref[...]
    k_scr[hist:hist + ts, :] = kc_ref[...]
    v_scr[0:hist, :] = vp_ref[...]
    v_scr[hist:hist + ts, :] = vc_ref[...]
    lane_q = lax.broadcasted_iota(jnp.int32, (CHUNK, LANES), 1)
    band_pos = lax.broadcasted_iota(jnp.int32, (1, BAND), 1)
    neg = jnp.finfo(F32).min
    pairs = ATTN_WIDTH // LANES

    def chunk_body(j, carry):
        r0 = pl.multiple_of(j * CHUNK, CHUNK)
        first_valid = jnp.maximum(LEFT_CHUNKS - (i * (ts // CHUNK) + j), 0) * CHUNK
        valid = band_pos >= first_valid
        for p in range(pairs):
            ls = slice(p * LANES, (p + 1) * LANES)
            q2 = q_ref[pl.ds(r0, CHUNK), ls]
            k2 = k_scr[pl.ds(r0, BAND), ls]
            v2 = v_scr[pl.ds(r0, BAND), ls]
            outs = []
            for a in range(2):
                qa = jnp.where((lane_q >> HEAD_SHIFT) == a, q2, jnp.zeros_like(q2))
                s = lax.dot_general(qa, k2, NT_DIMS, preferred_element_type=F32) + bias_ref[2 * p + a]
                s = jnp.where(valid, s, neg)
                e = jnp.exp(s - jnp.max(s, axis=-1, keepdims=True))
                denom = jnp.sum(e, axis=-1, keepdims=True)
                outs.append(_dot(e.astype(BF16), v2) / denom)
            o2 = jnp.where((lane_q >> HEAD_SHIFT) == 0, outs[0], outs[1])
            o_ref[pl.ds(r0, CHUNK), ls] = o2.astype(BF16)
        return carry

    lax.fori_loop(0, ts // CHUNK, chunk_body, 0)


def _attn_call(q, k, v, bias, batch, seq):
    t = q.shape[0]
    ts = SEQ_TILE
    assert ts == LEFT_CHUNKS * CHUNK
    nt = seq // ts
    cur = pl.BlockSpec((ts, ATTN_WIDTH), lambda b, i: (b * nt + i, 0))
    prev = pl.BlockSpec((ts, ATTN_WIDTH), lambda b, i: (jnp.maximum(b * nt + i - 1, 0), 0))
    return pl.pallas_call(
        _attn_kernel,
        grid=(batch, nt),
        in_specs=[cur, prev, cur, prev, cur, _resident(bias.shape)],
        out_specs=cur,
        out_shape=jax.ShapeDtypeStruct((t, ATTN_WIDTH), BF16),
        scratch_shapes=[pltpu.VMEM((2 * ts, ATTN_WIDTH), BF16), pltpu.VMEM((2 * ts, ATTN_WIDTH), BF16)],
        compiler_params=_params("arbitrary", "arbitrary"),
        name="chunk_attn",
    )(q, k, k, v, v, bias)


def _outproj_kernel(ys_ref, ya_ref, x_ref, mod_ref, ws_ref, wa_ref, o_ref):
    mix = _dot(ys_ref[...], ws_ref[...]) + _dot(ya_ref[...], wa_ref[...])
    o_ref[...] = x_ref[...] + mod_ref[0, 2:3, :] * mix


def _outproj_call(ys, ya, x, mod, ws, wa, seq):
    t, d = x.shape
    tm = ROW_TILE
    tiles_per_seq = seq // tm
    row = lambda w: pl.BlockSpec((tm, w), lambda i: (i, 0))
    return pl.pallas_call(
        _outproj_kernel,
        grid=(t // tm,),
        in_specs=[row(SSD_WIDTH), row(ATTN_WIDTH), row(d),
                  pl.BlockSpec((1, 6, d), lambda i: (i // tiles_per_seq, 0, 0)),
                  _resident(ws.shape), _resident(wa.shape)],
        out_specs=row(d),
        out_shape=jax.ShapeDtypeStruct((t, d), F32),
        compiler_params=_params("arbitrary"),
        name="out_proj",
    )(ys, ya, x, mod, ws, wa)


def _ffn_kernel(x_ref, mod_ref, nw_ref, wg_ref, wv_ref, cwg_ref, cwv_ref, wd_ref, fnw_ref, o_ref,
                h_scr, ug_scr, uv_scr, acc_scr, *, tiles_per_seq, final):
    tm = x_ref.shape[0]
    i = pl.program_id(0)
    x = x_ref[...]
    h = _rms_scale(x) * nw_ref[...]
    h = h * (1.0 + mod_ref[0, 4:5, :]) + mod_ref[0, 3:4, :]

    @pl.when(i % tiles_per_seq == 0)
    def _():
        h_scr[0:HALO, :] = jnp.zeros((HALO, D_MODEL), BF16)

    h_scr[HALO:HALO + tm, :] = h.astype(BF16)

    def conv(u_scr, cw_ref, c):
        cs = slice(c * FFN_CHUNK, (c + 1) * FFN_CHUNK)
        out = u_scr[HALO:HALO + tm, :] * cw_ref[FFN_CONV - 1:FFN_CONV, cs]
        for k in range(FFN_CONV - 1):
            off = HALO - (FFN_CONV - 1) + k
            out = out + u_scr[off:off + tm, :] * cw_ref[k:k + 1, cs]
        return out

    for c in range(FFN_DIM // FFN_CHUNK):
        cs = slice(c * FFN_CHUNK, (c + 1) * FFN_CHUNK)
        ug_scr[...] = _dot(h_scr[...], wg_ref[:, cs])
        uv_scr[...] = _dot(h_scr[...], wv_ref[:, cs])
        act = (_silu(conv(ug_scr, cwg_ref, c)) * conv(uv_scr, cwv_ref, c)).astype(BF16)
        part = _dot(act, wd_ref[cs, :])
        if c == 0:
            acc_scr[...] = part
        else:
            acc_scr[...] += part

    h_scr[0:HALO, :] = h_scr[tm:tm + HALO, :]
    out = x + mod_ref[0, 5:6, :] * acc_scr[...]
    if final:
        out = _rms_scale(out) * fnw_ref[...]
    o_ref[...] = out


def _ffn_call(x, mod, nw, wg, wv, cwg, cwv, wd, fnw, seq, final):
    t, d = x.shape
    tm = ROW_TILE
    tiles_per_seq = seq // tm
    row = pl.BlockSpec((tm, d), lambda i: (i, 0))
    return pl.pallas_call(
        functools.partial(_ffn_kernel, tiles_per_seq=tiles_per_seq, final=final),
        grid=(t // tm,),
        in_specs=[row, pl.BlockSpec((1, 6, d), lambda i: (i // tiles_per_seq, 0, 0)), _resident((1, d)),
                  _resident(wg.shape), _resident(wv.shape), _resident(cwg.shape), _resident(cwv.shape),
                  _resident(wd.shape), _resident((1, d))],
        out_specs=row,
        out_shape=jax.ShapeDtypeStruct((t, d), F32),
        scratch_shapes=[
            pltpu.VMEM((tm + HALO, d), BF16),
            pltpu.VMEM((tm + HALO, FFN_CHUNK), F32),
            pltpu.VMEM((tm + HALO, FFN_CHUNK), F32),
            pltpu.VMEM((tm, d), F32),
        ],
        compiler_params=_params("arbitrary"),
        name="conv_ffn",
    )(x, mod, nw, wg, wv, cwg, cwv, wd, fnw)


def _rel_bias_table(rel_bias):
    qi = np.arange(CHUNK)[:, None]
    kj = np.arange(BAND)[None, :]
    idx = np.clip(qi - kj + LEFT_CHUNKS * CHUNK, -MAX_REL, MAX_REL) + MAX_REL
    return rel_bias.astype(F32)[:, idx]


def _per_lane(v):
    return jnp.repeat(v.astype(F32), SSD_HEAD_DIM)[None, :]


def kernel(x, c, norm_mix_w, w_ada, b_ada, w_in, ssd_conv_w, ssd_conv_b, dt_bias, a_log, d_skip, ssd_norm_w,
           rel_bias, w_out, norm_ffn_w, w_up, ffn_conv_w, w_down, final_norm_w):
    batch, seq, d = x.shape
    depth = w_ada.shape[0]
    assert d == D_MODEL and seq % SEQ_TILE == 0 and seq % ROW_TILE == 0
    mod_all = _ada_call(c, w_ada, b_ada).reshape(depth, batch, 6, d)

    o_z, o_xbc, o_dt, o_q = 0, SSD_WIDTH, SSD_WIDTH + XBC_DIM, SSD_WIDTH + XBC_DIM + SSD_HEADS
    xf = x.reshape(batch * seq, d)
    for l in range(depth):
        mod = mod_all[l]
        w = w_in[l]
        wz = w[:, o_z:o_xbc].astype(BF16)
        wxbc = w[:, o_xbc:o_dt].astype(BF16)
        wdt = jnp.repeat(w[:, o_dt:o_q], SSD_HEAD_DIM, axis=1).astype(BF16)
        wqkv = w[:, o_q:].astype(BF16)
        z, xbc, dt, q, k, v = _inproj_call(xf, mod, norm_mix_w[l][None, :], wz, wxbc, wdt, wqkv, seq)
        y_ssd = _ssd_call(xbc, dt, z, ssd_conv_w[l], ssd_conv_b[l][None, :], _per_lane(dt_bias[l]),
                          _per_lane(a_log[l]), _per_lane(d_skip[l]), ssd_norm_w[l][None, :], batch, seq)
        y_att = _attn_call(q, k, v, _rel_bias_table(rel_bias[l]), batch, seq)
        wo = w_out[l].astype(BF16)
        xf = _outproj_call(y_ssd, y_att, xf, mod, wo[:SSD_WIDTH], wo[SSD_WIDTH:], seq)
        wu = w_up[l].astype(BF16)
        xf = _ffn_call(xf, mod, norm_ffn_w[l][None, :], wu[:, :FFN_DIM], wu[:, FFN_DIM:],
                       ffn_conv_w[l][:, :FFN_DIM], ffn_conv_w[l][:, FFN_DIM:], w_down[l].astype(BF16),
                       final_norm_w[None, :], seq, final=(l == depth - 1))
    return xf.reshape(batch, seq, d)
```

```python
import functools
import math

import numpy as np
import jax
import jax.numpy as jnp
from jax import lax
from jax.experimental import pallas as pl
from jax.experimental.pallas import tpu as pltpu

F32 = jnp.float32
BF16 = jnp.bfloat16

D_MODEL = 1024
CHUNK = 64
SSD_WIDTH = 512
SSD_HEAD_DIM = 64
SSD_HEADS = 8
SSD_GROUPS = 2
SSD_STATE = 128
SSD_CONV = 4
XBC_DIM = SSD_WIDTH + 2 * SSD_GROUPS * SSD_STATE
GROUP_WIDTH = SSD_WIDTH // SSD_GROUPS
ATTN_WIDTH = 512
ATTN_HEAD_DIM = 64
ATTN_HEADS = 8
LEFT_CHUNKS = 8
BAND = (LEFT_CHUNKS + 1) * CHUNK
MAX_REL = 128
FFN_DIM = 2816
FFN_CONV = 3
EPS = 1e-6
LOG2E = math.log2(math.e)

HEAD_SHIFT = 6
LANES = 128
SUBLANES = 8
ROW_TILE = 512
SEQ_TILE = 512
ATTN_GROUP_Q = 2 * CHUNK
ATTN_GROUP_K = ATTN_GROUP_Q + LEFT_CHUNKS * CHUNK
FFN_CHUNK = 256
HALO = 16
VMEM_LIMIT = 56 * 1024 * 1024

NT_DIMS = (((1,), (1,)), ((), ()))
TN_DIMS = (((0,), (0,)), ((), ()))


def _dot(a, b):
    return jnp.dot(a, b, preferred_element_type=F32)


def _silu(x):
    return x / (1.0 + jnp.exp(-x))


def _softplus(x):
    return jnp.maximum(x, 0.0) + jnp.log(1.0 + jnp.exp(-jnp.abs(x)))


def _rms_scale(x):
    return x * lax.rsqrt(jnp.mean(x * x, axis=-1, keepdims=True) + EPS)


def _params(*semantics):
    return pltpu.CompilerParams(dimension_semantics=semantics, vmem_limit_bytes=VMEM_LIMIT)


def _resident(shape):
    return pl.BlockSpec(shape, lambda *_: (0,) * len(shape), pipeline_mode=pl.Buffered(1))


def _ada_kernel(c_ref, w_ref, b_ref, o_ref):
    c = c_ref[...]
    c_act = _silu(c).astype(BF16)
    o_ref[0] = _dot(c_act, w_ref[0].astype(BF16)) + b_ref[0]


def _ada_call(c, w_ada, b_ada):
    depth, d, n = w_ada.shape
    b = c.shape[0]
    tn = n // 4
    return pl.pallas_call(
        _ada_kernel,
        grid=(depth, n // tn),
        in_specs=[
            pl.BlockSpec((b, d), lambda l, j: (0, 0)),
            pl.BlockSpec((1, d, tn), lambda l, j: (l, 0, j)),
            pl.BlockSpec((1, 1, tn), lambda l, j: (l, 0, j)),
        ],
        out_specs=pl.BlockSpec((1, b, tn), lambda l, j: (l, 0, j)),
        out_shape=jax.ShapeDtypeStruct((depth, b, n), F32),
        compiler_params=_params("arbitrary", "arbitrary"),
        name="ada_mod",
    )(c, w_ada, b_ada.reshape(depth, 1, n))


def _inproj_kernel(x_ref, mod_ref, nw_ref, wz_ref, wxbc_ref, wdt_ref, wq_ref, wkt_ref, wv_ref,
                   cw_ref, cb_ref, dtb_ref,
                   zs_ref, xs_ref, bc_ref, dt_ref, q_ref, kt_ref, v_ref, conv_scr, *, tiles_per_seq):
    tm = x_ref.shape[0]
    h = _rms_scale(x_ref[...]) * nw_ref[...]
    h = h * (1.0 + mod_ref[0, 1:2, :]) + mod_ref[0, 0:1, :]
    hb = h.astype(BF16)
    zs_ref[...] = _silu(_dot(hb, wz_ref[...]))

    @pl.when(pl.program_id(0) % tiles_per_seq == 0)
    def _():
        conv_scr[0:SUBLANES, :] = jnp.zeros((SUBLANES, XBC_DIM), F32)

    conv_scr[SUBLANES:SUBLANES + tm, :] = _dot(hb, wxbc_ref[...])
    acc = cb_ref[...] + conv_scr[SUBLANES:SUBLANES + tm, :] * cw_ref[SSD_CONV - 1:SSD_CONV, :]
    for k in range(SSD_CONV - 1):
        off = SUBLANES - (SSD_CONV - 1) + k
        acc = acc + conv_scr[off:off + tm, :] * cw_ref[k:k + 1, :]
    xc = _silu(acc)
    xs_ref[...] = xc[:, :SSD_WIDTH]
    bc_ref[...] = xc[:, SSD_WIDTH:].astype(BF16)
    conv_scr[0:SUBLANES, :] = conv_scr[tm:tm + SUBLANES, :]

    dt_ref[...] = _softplus(_dot(hb, wdt_ref[...]) + dtb_ref[...])
    q_ref[...] = (_dot(hb, wq_ref[...]) * (LOG2E * ATTN_HEAD_DIM ** -0.5)).astype(BF16)
    kt_ref[...] = lax.dot_general(wkt_ref[...], hb, NT_DIMS, preferred_element_type=F32).astype(BF16)
    v_ref[...] = _dot(hb, wv_ref[...]).astype(BF16)


def _inproj_call(x, mod, nw, wz, wxbc, wdt, wq, wkt, wv, cw, cb, dtb, seq):
    t, d = x.shape
    tm = ROW_TILE
    tiles_per_seq = seq // tm
    row = lambda w: pl.BlockSpec((tm, w), lambda i: (i, 0))
    col = pl.BlockSpec((ATTN_WIDTH, tm), lambda i: (0, i))
    sds = jax.ShapeDtypeStruct
    return pl.pallas_call(
        functools.partial(_inproj_kernel, tiles_per_seq=tiles_per_seq),
        grid=(t // tm,),
        in_specs=[
            row(d),
            pl.BlockSpec((1, 6, d), lambda i: (i // tiles_per_seq, 0, 0)),
            _resident((1, d)),
            _resident(wz.shape), _resident(wxbc.shape), _resident(wdt.shape),
            _resident(wq.shape), _resident(wkt.shape), _resident(wv.shape),
            _resident(cw.shape), _resident(cb.shape), _resident(dtb.shape),
        ],
        out_specs=[row(SSD_WIDTH), row(SSD_WIDTH), row(XBC_DIM - SSD_WIDTH), row(SSD_WIDTH),
                   row(ATTN_WIDTH), col, row(ATTN_WIDTH)],
        out_shape=[sds((t, SSD_WIDTH), F32), sds((t, SSD_WIDTH), F32), sds((t, XBC_DIM - SSD_WIDTH), BF16),
                   sds((t, SSD_WIDTH), F32),
                   sds((t, ATTN_WIDTH), BF16), sds((ATTN_WIDTH, t), BF16), sds((t, ATTN_WIDTH), BF16)],
        scratch_shapes=[pltpu.VMEM((tm + SUBLANES, XBC_DIM), F32)],
        compiler_params=_params("arbitrary"),
        name="in_proj",
    )(x, mod, nw, wz, wxbc, wdt, wq, wkt, wv, cw, cb, dtb)


def _split3(a):
    hi = a.astype(BF16)
    r1 = a - hi.astype(F32)
    mid = r1.astype(BF16)
    lo = (r1 - mid.astype(F32)).astype(BF16)
    return hi, mid, lo


def _ssd_kernel(xs_ref, bc_ref, dt_ref, zs_ref, alog_ref, dskip_ref, nw_ref, y_ref, state_scr):
    ts = xs_ref.shape[0]

    @pl.when(pl.program_id(1) == 0)
    def _():
        state_scr[...] = jnp.zeros_like(state_scr)

    a_neg = -jnp.exp(alog_ref[...])
    dskip = dskip_ref[...]
    nw = nw_ref[...]

    rows = lax.broadcasted_iota(jnp.int32, (CHUNK, CHUNK), 0)
    cols = lax.broadcasted_iota(jnp.int32, (CHUNK, CHUNK), 1)
    tri = (rows >= cols).astype(BF16)
    row_w = lax.broadcasted_iota(jnp.int32, (CHUNK, SSD_WIDTH), 0)
    lane_w = lax.broadcasted_iota(jnp.int32, (CHUNK, SSD_WIDTH), 1)
    pos_w = lane_w & (SSD_HEAD_DIM - 1)
    diag_w = row_w == pos_w
    causal_w = row_w >= pos_w
    lane_g = lax.broadcasted_iota(jnp.int32, (CHUNK, GROUP_WIDTH), 1)
    heads_per_group = SSD_HEADS // SSD_GROUPS

    for c in range(ts // CHUNK):
        rs = slice(c * CHUNK, (c + 1) * CHUNK)
        xs = xs_ref[rs, :]
        dt = dt_ref[rs, :]
        adt = dt * a_neg
        xdt = xs * dt
        a_cs = sum(_dot(tri, piece) for piece in _split3(adt))
        a_last = a_cs[CHUNK - 1:CHUNK, :]
        a_row = jnp.sum(jnp.where(diag_w, a_cs, 0.0), axis=0, keepdims=True)
        decay = jnp.exp(jnp.where(causal_w, a_cs - a_row, -jnp.inf))
        decay_in = jnp.exp(a_cs)
        decay_out = jnp.exp(a_last - a_cs)
        chunk_decay = jnp.exp(a_last)

        ys = []
        for g in range(SSD_GROUPS):
            gl = slice(g * GROUP_WIDTH, (g + 1) * GROUP_WIDTH)
            b_gb = bc_ref[rs, g * SSD_STATE:(g + 1) * SSD_STATE]
            c_gb = bc_ref[rs, (SSD_GROUPS + g) * SSD_STATE:(SSD_GROUPS + g + 1) * SSD_STATE]
            cb_rep = lax.dot_general(c_gb, jnp.concatenate([b_gb] * heads_per_group, axis=0), NT_DIMS,
                                     preferred_element_type=F32)
            m = (cb_rep * decay[:, gl]).astype(BF16)
            x_gb = xdt[:, gl].astype(BF16)
            x_blockdiag = jnp.concatenate(
                [jnp.where(lane_g >> HEAD_SHIFT == j, x_gb, jnp.zeros_like(x_gb)) for j in range(heads_per_group)],
                axis=0)
            y_diag = _dot(m, x_blockdiag)
            prev = state_scr[:, gl]
            y_off = _dot(c_gb, prev.astype(BF16)) * decay_in[:, gl]
            xd = (xdt[:, gl] * decay_out[:, gl]).astype(BF16)
            new_states = lax.dot_general(b_gb, xd, TN_DIMS, preferred_element_type=F32)
            state_scr[:, gl] = prev * chunk_decay[:, gl] + new_states
            ys.append(y_diag + y_off)
        y = (jnp.concatenate(ys, axis=1) + dskip * xs) * zs_ref[rs, :]
        y_ref[rs, :] = (_rms_scale(y) * nw).astype(BF16)


def _ssd_call(xs, bc, dt, zs, alog, dskip, nw, batch, seq):
    t = xs.shape[0]
    ts = SEQ_TILE
    nt = seq // ts
    tile = lambda w: pl.BlockSpec((ts, w), lambda b, i: (b * nt + i, 0))
    return pl.pallas_call(
        _ssd_kernel,
        grid=(batch, nt),
        in_specs=[tile(SSD_WIDTH), tile(XBC_DIM - SSD_WIDTH), tile(SSD_WIDTH), tile(SSD_WIDTH),
                  _resident(alog.shape), _resident(dskip.shape), _resident(nw.shape)],
        out_specs=tile(SSD_WIDTH),
        out_shape=jax.ShapeDtypeStruct((t, SSD_WIDTH), BF16),
        scratch_shapes=[pltpu.VMEM((SSD_STATE, SSD_WIDTH), F32)],
        compiler_params=_params("arbitrary", "arbitrary"),
        name="ssd_scan",
    )(xs, bc, dt, zs, alog, dskip, nw)


def _attend(q_ref, ktp_ref, ktc_ref, vp_ref, vc_ref, bias_ref, ya_scr, first_tile):
    ts = q_ref.shape[0]
    lane_head = lax.broadcasted_iota(jnp.int32, (ATTN_GROUP_Q, LANES), 1) >> HEAD_SHIFT
    band_pos = lax.broadcasted_iota(jnp.int32, (1, ATTN_GROUP_K), 1)
    neg = jnp.finfo(F32).min
    for g in range(ts // ATTN_GROUP_Q):
        lo = g * ATTN_GROUP_Q
        hi = lo + ATTN_GROUP_Q
        for p in range(ATTN_WIDTH // LANES):
            sl = slice(p * LANES, (p + 1) * LANES)
            kt = jnp.concatenate([ktp_ref[sl, lo:ts], ktc_ref[sl, 0:hi]], axis=1)
            v2 = jnp.concatenate([vp_ref[lo:ts, sl], vc_ref[0:hi, sl]], axis=0)
            q2 = q_ref[lo:hi, sl]
            zero = jnp.zeros_like(q2)
            qs = jnp.concatenate([jnp.where(lane_head == 0, q2, zero), jnp.where(lane_head == 1, q2, zero)], axis=0)
            s = _dot(qs, kt) + bias_ref[p]
            if first_tile:
                s = jnp.where(band_pos >= ts - lo, s, neg)
            e = jnp.exp2(s - jnp.max(s, axis=-1, keepdims=True))
            denom = jnp.sum(e, axis=-1, keepdims=True)
            o = _dot(e.astype(BF16), v2) / denom
            o2 = jnp.where(lane_head == 0, o[:ATTN_GROUP_Q], o[ATTN_GROUP_Q:])
            ya_scr[lo:hi, sl] = o2.astype(BF16)


def _attn_kernel(q_ref, ktp_ref, ktc_ref, vp_ref, vc_ref, bias_ref, ys_ref, x_ref, mod_ref, ws_ref, wa_ref,
                 o_ref, ya_scr):
    first_tile = pl.program_id(1) == 0

    @pl.when(first_tile)
    def _():
        _attend(q_ref, ktp_ref, ktc_ref, vp_ref, vc_ref, bias_ref, ya_scr, True)

    @pl.when(jnp.logical_not(first_tile))
    def _():
        _attend(q_ref, ktp_ref, ktc_ref, vp_ref, vc_ref, bias_ref, ya_scr, False)

    mix = _dot(ys_ref[...], ws_ref[...]) + _dot(ya_scr[...], wa_ref[...])
    o_ref[...] = x_ref[...] + mod_ref[0, 2:3, :] * mix


def _attn_call(q, kt, v, bias, ys, x, mod, ws, wa, batch, seq):
    t, d = x.shape
    ts = SEQ_TILE
    assert ts == LEFT_CHUNKS * CHUNK
    nt = seq // ts
    tile = lambda w: pl.BlockSpec((ts, w), lambda b, i: (b * nt + i, 0))
    prev = pl.BlockSpec((ts, ATTN_WIDTH), lambda b, i: (jnp.maximum(b * nt + i - 1, 0), 0))
    cur_t = pl.BlockSpec((ATTN_WIDTH, ts), lambda b, i: (0, b * nt + i))
    prev_t = pl.BlockSpec((ATTN_WIDTH, ts), lambda b, i: (0, jnp.maximum(b * nt + i - 1, 0)))
    return pl.pallas_call(
        _attn_kernel,
        grid=(batch, nt),
        in_specs=[tile(ATTN_WIDTH), prev_t, cur_t, prev, tile(ATTN_WIDTH), _resident(bias.shape),
                  tile(SSD_WIDTH), tile(d), pl.BlockSpec((1, 6, d), lambda b, i: (b, 0, 0)),
                  _resident(ws.shape), _resident(wa.shape)],
        out_specs=tile(d),
        out_shape=jax.ShapeDtypeStruct((t, d), F32),
        scratch_shapes=[pltpu.VMEM((ts, ATTN_WIDTH), BF16)],
        compiler_params=_params("arbitrary", "arbitrary"),
        name="attn_out",
    )(q, kt, kt, v, v, bias, ys, x, mod, ws, wa)


def _ffn_kernel(x_ref, mod_ref, nw_ref, wu_ref, cw_ref, wd_ref, fnw_ref, o_ref,
                h_scr, u_scr, act_scr, *, tiles_per_seq, final):
    tm = x_ref.shape[0]
    i = pl.program_id(0)
    x = x_ref[...]
    h = _rms_scale(x) * nw_ref[...]
    h = h * (1.0 + mod_ref[0, 4:5, :]) + mod_ref[0, 3:4, :]

    @pl.when(i % tiles_per_seq == 0)
    def _():
        h_scr[0:HALO, :] = jnp.zeros((HALO, D_MODEL), BF16)

    h_scr[HALO:HALO + tm, :] = h.astype(BF16)

    for c in range(FFN_DIM // FFN_CHUNK):
        buf = u_scr.at[c % 2]
        buf[...] = _dot(h_scr[...], wu_ref[c])
        u = buf[HALO:HALO + tm, :] * cw_ref[c, FFN_CONV - 1:FFN_CONV, :]
        for k in range(FFN_CONV - 1):
            off = HALO - (FFN_CONV - 1) + k
            u = u + buf[off:off + tm, :] * cw_ref[c, k:k + 1, :]
        act_scr[:, c * FFN_CHUNK:(c + 1) * FFN_CHUNK] = (_silu(u[:, :FFN_CHUNK]) * u[:, FFN_CHUNK:]).astype(BF16)

    h_scr[0:HALO, :] = h_scr[tm:tm + HALO, :]
    out = x + mod_ref[0, 5:6, :] * _dot(act_scr[...], wd_ref[...])
    if final:
        out = _rms_scale(out) * fnw_ref[...]
    o_ref[...] = out


def _ffn_call(x, mod, nw, wu, cw, wd, fnw, seq, final):
    t, d = x.shape
    tm = ROW_TILE
    tiles_per_seq = seq // tm
    row = pl.BlockSpec((tm, d), lambda i: (i, 0))
    return pl.pallas_call(
        functools.partial(_ffn_kernel, tiles_per_seq=tiles_per_seq, final=final),
        grid=(t // tm,),
        in_specs=[row, pl.BlockSpec((1, 6, d), lambda i: (i // tiles_per_seq, 0, 0)), _resident((1, d)),
                  _resident(wu.shape), _resident(cw.shape), _resident(wd.shape), _resident((1, d))],
        out_specs=row,
        out_shape=jax.ShapeDtypeStruct((t, d), F32),
        scratch_shapes=[
            pltpu.VMEM((tm + HALO, d), BF16),
            pltpu.VMEM((2, tm + HALO, 2 * FFN_CHUNK), F32),
            pltpu.VMEM((tm, FFN_DIM), BF16),
        ],
        compiler_params=_params("arbitrary"),
        name="conv_ffn",
    )(x, mod, nw, wu, cw, wd, fnw)


def _interleave_gate_value(a):
    n = FFN_DIM // FFN_CHUNK
    lead = a.shape[:-1]
    a = a.reshape(lead + (2, n, FFN_CHUNK))
    a = jnp.moveaxis(a, -2, 0)
    return a.reshape((n,) + lead + (2 * FFN_CHUNK,))


def _rel_bias_table(rel_bias):
    rb = rel_bias.astype(F32) * LOG2E
    heads = rb.shape[0]
    period = ATTN_GROUP_Q + ATTN_GROUP_K
    far = rb[:, 2 * MAX_REL:]
    flat_len = LEFT_CHUNKS * CHUNK - MAX_REL
    w = jnp.concatenate([jnp.repeat(far, flat_len, axis=1), rb[:, :0:-1], far,
                         jnp.repeat(far, ATTN_GROUP_Q - 1, axis=1)], axis=1)
    assert w.shape[1] == period
    toep = jnp.tile(w, (1, ATTN_GROUP_Q))[:, :ATTN_GROUP_Q * (period - 1)]
    toep = toep.reshape(heads, ATTN_GROUP_Q, period - 1)[:, :, :ATTN_GROUP_K]
    first = (np.arange(ATTN_GROUP_Q)[:, None] // CHUNK) * CHUNK
    pos = np.arange(ATTN_GROUP_K)[None, :]
    window = (pos >= first) & (pos < first + BAND)
    table = jnp.where(window[None], toep, -1e30)
    return table.reshape(heads // 2, 2 * ATTN_GROUP_Q, ATTN_GROUP_K)


def _per_lane(v):
    return jnp.repeat(v.astype(F32), SSD_HEAD_DIM)[None, :]


def kernel(x, c, norm_mix_w, w_ada, b_ada, w_in, ssd_conv_w, ssd_conv_b, dt_bias, a_log, d_skip, ssd_norm_w,
           rel_bias, w_out, norm_ffn_w, w_up, ffn_conv_w, w_down, final_norm_w):
    batch, seq, d = x.shape
    depth = w_ada.shape[0]
    assert d == D_MODEL and seq % SEQ_TILE == 0 and seq % ROW_TILE == 0
    mod_all = _ada_call(c, w_ada, b_ada).reshape(depth, batch, 6, d)

    o_z, o_xbc, o_dt, o_q = 0, SSD_WIDTH, SSD_WIDTH + XBC_DIM, SSD_WIDTH + XBC_DIM + SSD_HEADS
    xf = x.reshape(batch * seq, d)
    for l in range(depth):
        mod = mod_all[l]
        w = w_in[l]
        wz = w[:, o_z:o_xbc].astype(BF16)
        wxbc = w[:, o_xbc:o_dt].astype(BF16)
        wdt = jnp.repeat(w[:, o_dt:o_q], SSD_HEAD_DIM, axis=1).astype(BF16)
        wq = w[:, o_q:o_q + ATTN_WIDTH].astype(BF16)
        wkt = w[:, o_q + ATTN_WIDTH:o_q + 2 * ATTN_WIDTH].T.astype(BF16)
        wv = w[:, o_q + 2 * ATTN_WIDTH:].astype(BF16)
        zs, xs, bc, dt, q, kt, v = _inproj_call(
            xf, mod, norm_mix_w[l][None, :], wz, wxbc, wdt, wq, wkt, wv,
            ssd_conv_w[l], ssd_conv_b[l][None, :], _per_lane(dt_bias[l]), seq)
        y_ssd = _ssd_call(xs, bc, dt, zs, _per_lane(a_log[l]), _per_lane(d_skip[l]), ssd_norm_w[l][None, :],
                          batch, seq)
        wo = w_out[l].astype(BF16)
        xf = _attn_call(q, kt, v, _rel_bias_table(rel_bias[l]), y_ssd, xf, mod, wo[:SSD_WIDTH], wo[SSD_WIDTH:],
                        batch, seq)
        xf = _ffn_call(xf, mod, norm_ffn_w[l][None, :], _interleave_gate_value(w_up[l].astype(BF16)),
                       _interleave_gate_value(ffn_conv_w[l]), w_down[l].astype(BF16),
                       final_norm_w[None, :], seq, final=(l == depth - 1))
    return xf.reshape(batch, seq, d)
```

```python
import functools
import math

import numpy as np
import jax
import jax.numpy as jnp
from jax import lax
from jax.experimental import pallas as pl
from jax.experimental.pallas import tpu as pltpu

F32 = jnp.float32
BF16 = jnp.bfloat16

D_MODEL = 1024
CHUNK = 64
SSD_WIDTH = 512
SSD_HEAD_DIM = 64
SSD_HEADS = 8
SSD_GROUPS = 2
SSD_STATE = 128
SSD_CONV = 4
XBC_DIM = SSD_WIDTH + 2 * SSD_GROUPS * SSD_STATE
GROUP_WIDTH = SSD_WIDTH // SSD_GROUPS
ATTN_WIDTH = 512
ATTN_HEAD_DIM = 64
ATTN_HEADS = 8
LEFT_CHUNKS = 8
BAND = (LEFT_CHUNKS + 1) * CHUNK
MAX_REL = 128
FFN_DIM = 2816
FFN_CONV = 3
EPS = 1e-6
LOG2E = math.log2(math.e)

HEAD_SHIFT = 6
LANES = 128
SUBLANES = 8
ROW_TILE = 512
SEQ_TILE = 512
ATTN_GROUP_Q = 2 * CHUNK
ATTN_GROUP_K = ATTN_GROUP_Q + LEFT_CHUNKS * CHUNK
ATTN_SLAB_HEADS = 4
FFN_CHUNK = 256
HALO = 16
VMEM_LIMIT = 56 * 1024 * 1024

NT_DIMS = (((1,), (1,)), ((), ()))
TN_DIMS = (((0,), (0,)), ((), ()))


def _dot(a, b):
    return jnp.dot(a, b, preferred_element_type=F32)


def _silu(x):
    half = 0.5 * x
    return half + half * jnp.tanh(half)


def _softplus(x):
    return jnp.maximum(x, 0.0) + jnp.log(1.0 + jnp.exp(-jnp.abs(x)))


def _rms_scale(x):
    return x * lax.rsqrt(jnp.mean(x * x, axis=-1, keepdims=True) + EPS)


def _params(*semantics):
    return pltpu.CompilerParams(dimension_semantics=semantics, vmem_limit_bytes=VMEM_LIMIT)


def _resident(shape):
    return pl.BlockSpec(shape, lambda *_: (0,) * len(shape), pipeline_mode=pl.Buffered(1))


def _layer(arr, l, block=None, index=None):
    block = tuple(arr.shape[1:]) if block is None else block
    index = (0,) * len(block) if index is None else index
    return pl.BlockSpec((None,) + block, lambda *_: (l,) + index, pipeline_mode=pl.Buffered(1))


def _ada_kernel(c_ref, w_ref, b_ref, o_ref):
    c = c_ref[...]
    c_act = _silu(c).astype(BF16)
    o_ref[0] = _dot(c_act, w_ref[0].astype(BF16)) + b_ref[0]


def _ada_call(c, w_ada, b_ada):
    depth, d, n = w_ada.shape
    b = c.shape[0]
    tn = n // 4
    return pl.pallas_call(
        _ada_kernel,
        grid=(depth, n // tn),
        in_specs=[
            pl.BlockSpec((b, d), lambda l, j: (0, 0)),
            pl.BlockSpec((1, d, tn), lambda l, j: (l, 0, j)),
            pl.BlockSpec((1, 1, tn), lambda l, j: (l, 0, j)),
        ],
        out_specs=pl.BlockSpec((1, b, tn), lambda l, j: (l, 0, j)),
        out_shape=jax.ShapeDtypeStruct((depth, b, n), F32),
        compiler_params=_params("arbitrary", "arbitrary"),
        name="ada_mod",
    )(c, w_ada, b_ada.reshape(depth, 1, n))


def _inproj_kernel(x_ref, mod_ref, nw_ref, wz_ref, wxbc_ref, wdt_ref, wq_ref, wkt_ref, wv_ref,
                   cw_ref, cb_ref, dtb_ref,
                   zs_ref, xs_ref, bc_ref, dt_ref, q_ref, kt_ref, v_ref, conv_scr, *, tiles_per_seq):
    tm = x_ref.shape[0]
    h = _rms_scale(x_ref[...]) * nw_ref[...]
    h = h * (1.0 + mod_ref[0, 1:2, :]) + mod_ref[0, 0:1, :]
    hb = h.astype(BF16)

    zs_ref[...] = _silu(_dot(hb, wz_ref[...]))

    @pl.when(pl.program_id(0) % tiles_per_seq == 0)
    def _():
        conv_scr[0:SUBLANES, :] = jnp.zeros((SUBLANES, XBC_DIM), F32)

    conv_scr[SUBLANES:SUBLANES + tm, :] = _dot(hb, wxbc_ref[...])
    acc = cb_ref[...] + conv_scr[SUBLANES:SUBLANES + tm, :] * cw_ref[SSD_CONV - 1:SSD_CONV, :]
    for k in range(SSD_CONV - 1):
        off = SUBLANES - (SSD_CONV - 1) + k
        acc = acc + conv_scr[off:off + tm, :] * cw_ref[k:k + 1, :]
    xc = _silu(acc)
    xs_ref[...] = xc[:, :SSD_WIDTH]
    bc_ref[...] = xc[:, SSD_WIDTH:].astype(BF16)
    conv_scr[0:SUBLANES, :] = conv_scr[tm:tm + SUBLANES, :]

    dt_ref[...] = _softplus(_dot(hb, wdt_ref[...]) + dtb_ref[...])
    q_ref[...] = (_dot(hb, wq_ref[...]) * (LOG2E * ATTN_HEAD_DIM ** -0.5)).astype(BF16)
    kt_ref[...] = lax.dot_general(wkt_ref[...], hb, NT_DIMS, preferred_element_type=F32).astype(BF16)
    v_ref[...] = _dot(hb, wv_ref[...]).astype(BF16)


def _inproj_call(l, x, mod, nw, wz, wxbc, wdt, wq, wkt, wv, cw, cb, dtb, seq):
    t, d = x.shape
    tm = ROW_TILE
    tiles_per_seq = seq // tm
    row = lambda w: pl.BlockSpec((tm, w), lambda i: (i, 0))
    col = pl.BlockSpec((ATTN_WIDTH, tm), lambda i: (0, i))
    sds = jax.ShapeDtypeStruct
    return pl.pallas_call(
        functools.partial(_inproj_kernel, tiles_per_seq=tiles_per_seq),
        grid=(t // tm,),
        in_specs=[
            row(d),
            pl.BlockSpec((None, 1, 6, d), lambda i: (l, i // tiles_per_seq, 0, 0)),
            _layer(nw, l),
            _layer(wz, l), _layer(wxbc, l), _layer(wdt, l), _layer(wq, l), _layer(wkt, l), _layer(wv, l),
            _layer(cw, l), _layer(cb, l), _layer(dtb, l),
        ],
        out_specs=[row(SSD_WIDTH), row(SSD_WIDTH), row(XBC_DIM - SSD_WIDTH), row(SSD_WIDTH),
                   row(ATTN_WIDTH), col, row(ATTN_WIDTH)],
        out_shape=[sds((t, SSD_WIDTH), F32), sds((t, SSD_WIDTH), F32), sds((t, XBC_DIM - SSD_WIDTH), BF16),
                   sds((t, SSD_WIDTH), F32),
                   sds((t, ATTN_WIDTH), BF16), sds((ATTN_WIDTH, t), BF16), sds((t, ATTN_WIDTH), BF16)],
        scratch_shapes=[pltpu.VMEM((tm + SUBLANES, XBC_DIM), F32)],
        compiler_params=_params("arbitrary"),
        name="in_proj",
    )(x, mod, nw, wz, wxbc, wdt, wq, wkt, wv, cw, cb, dtb)


def _split3(a):
    hi = a.astype(BF16)
    r1 = a - hi.astype(F32)
    mid = r1.astype(BF16)
    lo = (r1 - mid.astype(F32)).astype(BF16)
    return hi, mid, lo


def _ssd_kernel(xs_ref, bc_ref, dt_ref, zs_ref, alog_ref, dskip_ref, nw_ref, y_ref, state_scr):
    ts = xs_ref.shape[0]

    @pl.when(pl.program_id(1) == 0)
    def _():
        state_scr[...] = jnp.zeros_like(state_scr)

    a_neg = -jnp.exp(alog_ref[...])
    dskip = dskip_ref[...]
    nw = nw_ref[...]

    rows = lax.broadcasted_iota(jnp.int32, (CHUNK, CHUNK), 0)
    cols = lax.broadcasted_iota(jnp.int32, (CHUNK, CHUNK), 1)
    tri = (rows >= cols).astype(BF16)
    row_w = lax.broadcasted_iota(jnp.int32, (CHUNK, SSD_WIDTH), 0)
    lane_w = lax.broadcasted_iota(jnp.int32, (CHUNK, SSD_WIDTH), 1)
    pos_w = lane_w & (SSD_HEAD_DIM - 1)
    diag_w = row_w == pos_w
    causal_w = row_w >= pos_w
    lane_g = lax.broadcasted_iota(jnp.int32, (CHUNK, GROUP_WIDTH), 1)
    heads_per_group = SSD_HEADS // SSD_GROUPS

    for c in range(ts // CHUNK):
        rs = slice(c * CHUNK, (c + 1) * CHUNK)
        xs = xs_ref[rs, :]
        dt = dt_ref[rs, :]
        adt = dt * a_neg
        xdt = xs * dt
        a_cs = sum(_dot(tri, piece) for piece in _split3(adt))
        a_last = a_cs[CHUNK - 1:CHUNK, :]
        a_row = jnp.sum(jnp.where(diag_w, a_cs, 0.0), axis=0, keepdims=True)
        decay = jnp.exp(jnp.where(causal_w, a_cs - a_row, -jnp.inf))
        decay_in = jnp.exp(a_cs)
        decay_out = jnp.exp(a_last - a_cs)
        chunk_decay = jnp.exp(a_last)

        ys = []
        for g in range(SSD_GROUPS):
            gl = slice(g * GROUP_WIDTH, (g + 1) * GROUP_WIDTH)
            b_gb = bc_ref[rs, g * SSD_STATE:(g + 1) * SSD_STATE]
            c_gb = bc_ref[rs, (SSD_GROUPS + g) * SSD_STATE:(SSD_GROUPS + g + 1) * SSD_STATE]
            cb_rep = lax.dot_general(c_gb, jnp.concatenate([b_gb] * heads_per_group, axis=0), NT_DIMS,
                                     preferred_element_type=F32)
            m = (cb_rep * decay[:, gl]).astype(BF16)
            x_gb = xdt[:, gl].astype(BF16)
            x_blockdiag = jnp.concatenate(
                [jnp.where(lane_g >> HEAD_SHIFT == j, x_gb, jnp.zeros_like(x_gb)) for j in range(heads_per_group)],
                axis=0)
            y_diag = _dot(m, x_blockdiag)
            prev = state_scr[:, gl]
            y_off = _dot(c_gb, prev.astype(BF16)) * decay_in[:, gl]
            xd = (xdt[:, gl] * decay_out[:, gl]).astype(BF16)
            new_states = lax.dot_general(b_gb, xd, TN_DIMS, preferred_element_type=F32)
            state_scr[:, gl] = prev * chunk_decay[:, gl] + new_states
            ys.append(y_diag + y_off)
        y = (jnp.concatenate(ys, axis=1) + dskip * xs) * zs_ref[rs, :]
        y_ref[rs, :] = (_rms_scale(y) * nw).astype(BF16)


def _ssd_call(l, xs, bc, dt, zs, alog, dskip, nw, batch, seq):
    t = xs.shape[0]
    ts = SEQ_TILE
    nt = seq // ts
    tile = lambda w: pl.BlockSpec((ts, w), lambda b, i: (b * nt + i, 0))
    return pl.pallas_call(
        _ssd_kernel,
        grid=(batch, nt),
        in_specs=[tile(SSD_WIDTH), tile(XBC_DIM - SSD_WIDTH), tile(SSD_WIDTH), tile(SSD_WIDTH),
                  _layer(alog, l), _layer(dskip, l), _layer(nw, l)],
        out_specs=tile(SSD_WIDTH),
        out_shape=jax.ShapeDtypeStruct((t, SSD_WIDTH), BF16),
        scratch_shapes=[pltpu.VMEM((SSD_STATE, SSD_WIDTH), F32)],
        compiler_params=_params("arbitrary", "arbitrary"),
        name="ssd_scan",
    )(xs, bc, dt, zs, alog, dskip, nw)


def _attend(q_ref, ktp_ref, ktc_ref, vp_ref, vc_ref, bias_ref, ya_scr, first_tile):
    ts = q_ref.shape[0]
    slab = ATTN_SLAB_HEADS * ATTN_HEAD_DIM
    lane_head = lax.broadcasted_iota(jnp.int32, (ATTN_GROUP_Q, slab), 1) >> HEAD_SHIFT
    band_pos = lax.broadcasted_iota(jnp.int32, (1, ATTN_GROUP_K), 1)
    neg = jnp.finfo(F32).min
    for g in range(ts // ATTN_GROUP_Q):
        lo = g * ATTN_GROUP_Q
        hi = lo + ATTN_GROUP_Q
        for p in range(ATTN_WIDTH // slab):
            sl = slice(p * slab, (p + 1) * slab)
            kt = jnp.concatenate([ktp_ref[sl, lo:ts], ktc_ref[sl, 0:hi]], axis=1)
            vs = jnp.concatenate([vp_ref[lo:ts, sl], vc_ref[0:hi, sl]], axis=0)
            qg = q_ref[lo:hi, sl]
            zero = jnp.zeros_like(qg)
            qs = jnp.concatenate([jnp.where(lane_head == a, qg, zero) for a in range(ATTN_SLAB_HEADS)], axis=0)
            s = _dot(qs, kt) + bias_ref[p]
            if first_tile:
                s = jnp.where(band_pos >= ts - lo, s, neg)
            e = jnp.exp2(s - jnp.max(s, axis=-1, keepdims=True))
            denom = jnp.sum(e, axis=-1, keepdims=True)
            o = _dot(e.astype(BF16), vs) / denom
            og = o[:ATTN_GROUP_Q]
            for a in range(1, ATTN_SLAB_HEADS):
                og = jnp.where(lane_head == a, o[a * ATTN_GROUP_Q:(a + 1) * ATTN_GROUP_Q], og)
            ya_scr[lo:hi, sl] = og.astype(BF16)


def _attn_kernel(q_ref, ktp_ref, ktc_ref, vp_ref, vc_ref, bias_ref, ys_ref, x_ref, mod_ref, ws_ref, wa_ref,
                 o_ref, ya_scr):
    first_tile = pl.program_id(1) == 0

    @pl.when(first_tile)
    def _():
        _attend(q_ref, ktp_ref, ktc_ref, vp_ref, vc_ref, bias_ref, ya_scr, True)

    @pl.when(jnp.logical_not(first_tile))
    def _():
        _attend(q_ref, ktp_ref, ktc_ref, vp_ref, vc_ref, bias_ref, ya_scr, False)

    mix = _dot(ys_ref[...], ws_ref[...]) + _dot(ya_scr[...], wa_ref[...])
    o_ref[...] = x_ref[...] + mod_ref[0, 2:3, :] * mix


def _attn_call(l, q, kt, v, bias, ys, x, mod, wo, batch, seq):
    t, d = x.shape
    ts = SEQ_TILE
    assert ts == LEFT_CHUNKS * CHUNK
    nt = seq // ts
    tile = lambda w: pl.BlockSpec((ts, w), lambda b, i: (b * nt + i, 0))
    prev = pl.BlockSpec((ts, ATTN_WIDTH), lambda b, i: (jnp.maximum(b * nt + i - 1, 0), 0))
    cur_t = pl.BlockSpec((ATTN_WIDTH, ts), lambda b, i: (0, b * nt + i))
    prev_t = pl.BlockSpec((ATTN_WIDTH, ts), lambda b, i: (0, jnp.maximum(b * nt + i - 1, 0)))
    return pl.pallas_call(
        _attn_kernel,
        grid=(batch, nt),
        in_specs=[tile(ATTN_WIDTH), prev_t, cur_t, prev, tile(ATTN_WIDTH), _layer(bias, l),
                  tile(SSD_WIDTH), tile(d), pl.BlockSpec((None, 1, 6, d), lambda b, i: (l, b, 0, 0)),
                  _layer(wo, l, (SSD_WIDTH, d), (0, 0)), _layer(wo, l, (ATTN_WIDTH, d), (1, 0))],
        out_specs=tile(d),
        out_shape=jax.ShapeDtypeStruct((t, d), F32),
        scratch_shapes=[pltpu.VMEM((ts, ATTN_WIDTH), BF16)],
        compiler_params=_params("arbitrary", "arbitrary"),
        name="attn_out",
    )(q, kt, kt, v, v, bias, ys, x, mod, wo, wo)


def _ffn_kernel(x_ref, mod_ref, nw_ref, wu_ref, cw_ref, wd_ref, fnw_ref, o_ref,
                h_scr, u_scr, act_scr, *, tiles_per_seq, final):
    tm = x_ref.shape[0]
    i = pl.program_id(0)
    x = x_ref[...]
    h = _rms_scale(x) * nw_ref[...]
    h = h * (1.0 + mod_ref[0, 4:5, :]) + mod_ref[0, 3:4, :]

    @pl.when(i % tiles_per_seq == 0)
    def _():
        h_scr[0:HALO, :] = jnp.zeros((HALO, D_MODEL), BF16)

    h_scr[HALO:HALO + tm, :] = h.astype(BF16)

    for c in range(FFN_DIM // FFN_CHUNK):
        buf = u_scr.at[c % 2]
        gate = slice(c * FFN_CHUNK, (c + 1) * FFN_CHUNK)
        value = slice(FFN_DIM + c * FFN_CHUNK, FFN_DIM + (c + 1) * FFN_CHUNK)
        pick = lambda ref, rows: jnp.concatenate([ref[rows, gate], ref[rows, value]], axis=1)
        buf[...] = _dot(h_scr[...], pick(wu_ref, slice(None)))
        u = buf[HALO:HALO + tm, :] * pick(cw_ref, slice(FFN_CONV - 1, FFN_CONV))
        for k in range(FFN_CONV - 1):
            off = HALO - (FFN_CONV - 1) + k
            u = u + buf[off:off + tm, :] * pick(cw_ref, slice(k, k + 1))
        act_scr[:, gate] = (_silu(u[:, :FFN_CHUNK]) * u[:, FFN_CHUNK:]).astype(BF16)

    h_scr[0:HALO, :] = h_scr[tm:tm + HALO, :]
    out = x + mod_ref[0, 5:6, :] * _dot(act_scr[...], wd_ref[...])
    if final:
        out = _rms_scale(out) * fnw_ref[...]
    o_ref[...] = out


def _ffn_call(l, x, mod, nw, wu, cw, wd, fnw, seq, final):
    t, d = x.shape
    tm = ROW_TILE
    tiles_per_seq = seq // tm
    row = pl.BlockSpec((tm, d), lambda i: (i, 0))
    return pl.pallas_call(
        functools.partial(_ffn_kernel, tiles_per_seq=tiles_per_seq, final=final),
        grid=(t // tm,),
        in_specs=[row, pl.BlockSpec((None, 1, 6, d), lambda i: (l, i // tiles_per_seq, 0, 0)), _layer(nw, l),
                  _layer(wu, l), _layer(cw, l), _layer(wd, l), _resident((1, d))],
        out_specs=row,
        out_shape=jax.ShapeDtypeStruct((t, d), F32),
        scratch_shapes=[
            pltpu.VMEM((tm + HALO, d), BF16),
            pltpu.VMEM((2, tm + HALO, 2 * FFN_CHUNK), F32),
            pltpu.VMEM((tm, FFN_DIM), BF16),
        ],
        compiler_params=_params("arbitrary"),
        name="conv_ffn",
    )(x, mod, nw, wu, cw, wd, fnw)


def _rel_bias_table(rel_bias):
    depth, heads = rel_bias.shape[:2]
    rb = rel_bias.astype(F32).reshape(depth * heads, -1) * LOG2E
    period = ATTN_GROUP_Q + ATTN_GROUP_K
    far = rb[:, 2 * MAX_REL:]
    flat_len = LEFT_CHUNKS * CHUNK - MAX_REL
    unseen = period - flat_len - 2 * MAX_REL
    w = jnp.concatenate([jnp.repeat(far, flat_len, axis=1), rb[:, :0:-1], jnp.repeat(far, unseen, axis=1)], axis=1)
    assert w.shape[1] == period
    toep = jnp.tile(w, (1, ATTN_GROUP_Q))[:, :ATTN_GROUP_Q * (period - 1)]
    toep = toep.reshape(depth * heads, ATTN_GROUP_Q, period - 1)[:, :, :ATTN_GROUP_K]
    first = (np.arange(ATTN_GROUP_Q)[:, None] // CHUNK) * CHUNK
    pos = np.arange(ATTN_GROUP_K)[None, :]
    window = (pos >= first) & (pos < first + BAND)
    table = jnp.where(window[None], toep, -1e30)
    return table.reshape(depth, heads // ATTN_SLAB_HEADS, ATTN_SLAB_HEADS * ATTN_GROUP_Q, ATTN_GROUP_K)


def _per_lane(v):
    return jnp.repeat(v.astype(F32), SSD_HEAD_DIM, axis=1)[:, None, :]


def kernel(x, c, norm_mix_w, w_ada, b_ada, w_in, ssd_conv_w, ssd_conv_b, dt_bias, a_log, d_skip, ssd_norm_w,
           rel_bias, w_out, norm_ffn_w, w_up, ffn_conv_w, w_down, final_norm_w):
    batch, seq, d = x.shape
    depth = w_ada.shape[0]
    assert d == D_MODEL and seq % SEQ_TILE == 0 and seq % ROW_TILE == 0
    mod = _ada_call(c, w_ada, b_ada).reshape(depth, batch, 6, d)

    o_xbc, o_dt, o_q = SSD_WIDTH, SSD_WIDTH + XBC_DIM, SSD_WIDTH + XBC_DIM + SSD_HEADS
    o_k, o_v = o_q + ATTN_WIDTH, o_q + 2 * ATTN_WIDTH
    w_in_b = w_in.astype(BF16)
    wz, wxbc, wq, wv = w_in_b[:, :, :o_xbc], w_in_b[:, :, o_xbc:o_dt], w_in_b[:, :, o_q:o_k], w_in_b[:, :, o_v:]
    wdt = jnp.repeat(w_in_b[:, :, o_dt:o_q], SSD_HEAD_DIM, axis=2)
    wkt = jnp.swapaxes(w_in_b[:, :, o_k:o_v], 1, 2)
    wo, wu, wd = w_out.astype(BF16), w_up.astype(BF16), w_down.astype(BF16)
    row3 = lambda a: a[:, None, :]
    nmw, nfw, cb, snw = row3(norm_mix_w), row3(norm_ffn_w), row3(ssd_conv_b), row3(ssd_norm_w)
    dtb, alog, dskip = _per_lane(dt_bias), _per_lane(a_log), _per_lane(d_skip)
    bias = _rel_bias_table(rel_bias)

    xf = x.reshape(batch * seq, d)
    for l in range(depth):
        zs, xs, bc, dt, q, kt, v = _inproj_call(l, xf, mod, nmw, wz, wxbc, wdt, wq, wkt, wv, ssd_conv_w, cb, dtb, seq)
        y_ssd = _ssd_call(l, xs, bc, dt, zs, alog, dskip, snw, batch, seq)
        xf = _attn_call(l, q, kt, v, bias, y_ssd, xf, mod, wo, batch, seq)
        xf = _ffn_call(l, xf, mod, nfw, wu, ffn_conv_w, wd, final_norm_w[None, :], seq, final=(l == depth - 1))
    return xf.reshape(batch, seq, d)
```

```python
import functools
import math

import numpy as np
import jax
import jax.numpy as jnp
from jax import lax
from jax.experimental import pallas as pl
from jax.experimental.pallas import tpu as pltpu

F32 = jnp.float32
BF16 = jnp.bfloat16

D_MODEL = 1024
CHUNK = 64
SSD_WIDTH = 512
SSD_HEAD_DIM = 64
SSD_HEADS = 8
SSD_GROUPS = 2
SSD_STATE = 128
SSD_CONV = 4
XBC_DIM = SSD_WIDTH + 2 * SSD_GROUPS * SSD_STATE
GROUP_WIDTH = SSD_WIDTH // SSD_GROUPS
ATTN_WIDTH = 512
ATTN_HEAD_DIM = 64
ATTN_HEADS = 8
LEFT_CHUNKS = 8
BAND = (LEFT_CHUNKS + 1) * CHUNK
MAX_REL = 128
FFN_DIM = 2816
FFN_CONV = 3
EPS = 1e-6
LOG2E = math.log2(math.e)

HEAD_SHIFT = 6
LANES = 128
SUBLANES = 8
ROW_TILE = 512
SEQ_TILE = 512
ATTN_GROUP_Q = 2 * CHUNK
ATTN_GROUP_K = ATTN_GROUP_Q + LEFT_CHUNKS * CHUNK
ATTN_SLAB_HEADS = 4
FFN_CHUNK = 256
HALO = 16
VMEM_LIMIT = 56 * 1024 * 1024

NT_DIMS = (((1,), (1,)), ((), ()))
TN_DIMS = (((0,), (0,)), ((), ()))


def _dot(a, b):
    return jnp.dot(a, b, preferred_element_type=F32)


def _silu(x):
    half = 0.5 * x
    return half + half * jnp.tanh(half)


def _softplus(x):
    return jnp.maximum(x, 0.0) + jnp.log(1.0 + jnp.exp(-jnp.abs(x)))


def _rms_scale(x):
    return x * lax.rsqrt(jnp.mean(x * x, axis=-1, keepdims=True) + EPS)


def _params(*semantics):
    return pltpu.CompilerParams(dimension_semantics=semantics, vmem_limit_bytes=VMEM_LIMIT)


def _resident(shape):
    return pl.BlockSpec(shape, lambda *_: (0,) * len(shape), pipeline_mode=pl.Buffered(1))


def _layer(arr, l, block=None, index=None):
    block = tuple(arr.shape[1:]) if block is None else block
    index = (0,) * len(block) if index is None else index
    return pl.BlockSpec((None,) + block, lambda *_: (l,) + index, pipeline_mode=pl.Buffered(1))


def _ada_kernel(c_ref, w_ref, b_ref, o_ref):
    c = c_ref[...]
    c_act = _silu(c).astype(BF16)
    o_ref[0] = _dot(c_act, w_ref[0].astype(BF16)) + b_ref[0]


def _ada_call(c, w_ada, b_ada):
    depth, d, n = w_ada.shape
    b = c.shape[0]
    tn = n // 4
    return pl.pallas_call(
        _ada_kernel,
        grid=(depth, n // tn),
        in_specs=[
            pl.BlockSpec((b, d), lambda l, j: (0, 0)),
            pl.BlockSpec((1, d, tn), lambda l, j: (l, 0, j)),
            pl.BlockSpec((1, 1, tn), lambda l, j: (l, 0, j)),
        ],
        out_specs=pl.BlockSpec((1, b, tn), lambda l, j: (l, 0, j)),
        out_shape=jax.ShapeDtypeStruct((depth, b, n), F32),
        compiler_params=_params("arbitrary", "arbitrary"),
        name="ada_mod",
    )(c, w_ada, b_ada.reshape(depth, 1, n))


def _inproj_kernel(x_ref, mod_ref, nw_ref, wz_ref, wxbc_ref, wdt_ref, wq_ref, wkt_ref, wv_ref,
                   cw_ref, cb_ref, dtb_ref,
                   zs_ref, xs_ref, bc_ref, dt_ref, q_ref, kt_ref, v_ref, conv_scr, *, tiles_per_seq):
    tm = x_ref.shape[0]
    h = _rms_scale(x_ref[...]) * nw_ref[...]
    h = h * (1.0 + mod_ref[0, 1:2, :]) + mod_ref[0, 0:1, :]
    hb = h.astype(BF16)

    zs_ref[...] = _silu(_dot(hb, wz_ref[...]))

    @pl.when(pl.program_id(0) % tiles_per_seq == 0)
    def _():
        conv_scr[...] = jnp.zeros((SUBLANES, XBC_DIM), F32)

    xbc = _dot(hb, wxbc_ref[...])
    ext = jnp.concatenate([conv_scr[...], xbc], axis=0)
    acc = cb_ref[...] + xbc * cw_ref[SSD_CONV - 1:SSD_CONV, :]
    for k in range(SSD_CONV - 1):
        off = SUBLANES - (SSD_CONV - 1) + k
        acc = acc + ext[off:off + tm, :] * cw_ref[k:k + 1, :]
    xc = _silu(acc)
    xs_ref[...] = xc[:, :SSD_WIDTH]
    bc_ref[...] = xc[:, SSD_WIDTH:].astype(BF16)
    conv_scr[...] = xbc[tm - SUBLANES:tm, :]

    dt_ref[...] = _softplus(_dot(hb, wdt_ref[...]) + dtb_ref[...])
    q_ref[...] = (_dot(hb, wq_ref[...]) * (LOG2E * ATTN_HEAD_DIM ** -0.5)).astype(BF16)
    kt_ref[...] = lax.dot_general(wkt_ref[...], hb, NT_DIMS, preferred_element_type=F32).astype(BF16)
    v_ref[...] = _dot(hb, wv_ref[...]).astype(BF16)


def _inproj_call(l, x, mod, nw, wz, wxbc, wdt, wq, wkt, wv, cw, cb, dtb, seq):
    t, d = x.shape
    tm = ROW_TILE
    tiles_per_seq = seq // tm
    row = lambda w: pl.BlockSpec((tm, w), lambda i: (i, 0))
    col = pl.BlockSpec((ATTN_WIDTH, tm), lambda i: (0, i))
    sds = jax.ShapeDtypeStruct
    return pl.pallas_call(
        functools.partial(_inproj_kernel, tiles_per_seq=tiles_per_seq),
        grid=(t // tm,),
        in_specs=[
            row(d),
            pl.BlockSpec((None, 1, 6, d), lambda i: (l, i // tiles_per_seq, 0, 0)),
            _layer(nw, l),
            _layer(wz, l), _layer(wxbc, l), _layer(wdt, l), _layer(wq, l), _layer(wkt, l), _layer(wv, l),
            _layer(cw, l), _layer(cb, l), _layer(dtb, l),
        ],
        out_specs=[row(SSD_WIDTH), row(SSD_WIDTH), row(XBC_DIM - SSD_WIDTH), row(SSD_WIDTH),
                   row(ATTN_WIDTH), col, row(ATTN_WIDTH)],
        out_shape=[sds((t, SSD_WIDTH), F32), sds((t, SSD_WIDTH), F32), sds((t, XBC_DIM - SSD_WIDTH), BF16),
                   sds((t, SSD_WIDTH), F32),
                   sds((t, ATTN_WIDTH), BF16), sds((ATTN_WIDTH, t), BF16), sds((t, ATTN_WIDTH), BF16)],
        scratch_shapes=[pltpu.VMEM((SUBLANES, XBC_DIM), F32)],
        compiler_params=_params("arbitrary"),
        name="in_proj",
    )(x, mod, nw, wz, wxbc, wdt, wq, wkt, wv, cw, cb, dtb)


def _split3(a):
    hi = a.astype(BF16)
    r1 = a - hi.astype(F32)
    mid = r1.astype(BF16)
    lo = (r1 - mid.astype(F32)).astype(BF16)
    return hi, mid, lo


def _ssd_chunk(c, xs_ref, bc_ref, dt_ref, zs_ref, a_neg, dskip, nw, ys_scr, state_scr):
    rows = lax.broadcasted_iota(jnp.int32, (CHUNK, CHUNK), 0)
    cols = lax.broadcasted_iota(jnp.int32, (CHUNK, CHUNK), 1)
    tri = (rows >= cols).astype(BF16)
    row_w = lax.broadcasted_iota(jnp.int32, (CHUNK, SSD_WIDTH), 0)
    lane_w = lax.broadcasted_iota(jnp.int32, (CHUNK, SSD_WIDTH), 1)
    pos_w = lane_w & (SSD_HEAD_DIM - 1)
    diag_w = row_w == pos_w
    causal_w = row_w >= pos_w
    lane_g = lax.broadcasted_iota(jnp.int32, (CHUNK, GROUP_WIDTH), 1)
    heads_per_group = SSD_HEADS // SSD_GROUPS

    rs = slice(c * CHUNK, (c + 1) * CHUNK)
    xs = xs_ref[rs, :]
    dt = dt_ref[rs, :]
    adt = dt * a_neg
    xdt = xs * dt
    a_cs = sum(_dot(tri, piece) for piece in _split3(adt))
    a_last = a_cs[CHUNK - 1:CHUNK, :]
    a_row = jnp.sum(jnp.where(diag_w, a_cs, 0.0), axis=0, keepdims=True)
    decay = jnp.exp(jnp.where(causal_w, a_cs - a_row, -jnp.inf))
    decay_in = jnp.exp(a_cs)
    decay_out = jnp.exp(a_last - a_cs)
    chunk_decay = jnp.exp(a_last)

    ys = []
    for g in range(SSD_GROUPS):
        gl = slice(g * GROUP_WIDTH, (g + 1) * GROUP_WIDTH)
        b_gb = bc_ref[rs, g * SSD_STATE:(g + 1) * SSD_STATE]
        c_gb = bc_ref[rs, (SSD_GROUPS + g) * SSD_STATE:(SSD_GROUPS + g + 1) * SSD_STATE]
        cb_rep = lax.dot_general(c_gb, jnp.concatenate([b_gb] * heads_per_group, axis=0), NT_DIMS,
                                 preferred_element_type=F32)
        m = (cb_rep * decay[:, gl]).astype(BF16)
        x_gb = xdt[:, gl].astype(BF16)
        x_blockdiag = jnp.concatenate(
            [jnp.where(lane_g >> HEAD_SHIFT == j, x_gb, jnp.zeros_like(x_gb)) for j in range(heads_per_group)],
            axis=0)
        y_diag = _dot(m, x_blockdiag)
        prev = state_scr[:, gl]
        y_off = _dot(c_gb, prev.astype(BF16)) * decay_in[:, gl]
        xd = (xdt[:, gl] * decay_out[:, gl]).astype(BF16)
        new_states = lax.dot_general(b_gb, xd, TN_DIMS, preferred_element_type=F32)
        state_scr[:, gl] = prev * chunk_decay[:, gl] + new_states
        ys.append(y_diag + y_off)
    y = (jnp.concatenate(ys, axis=1) + dskip * xs) * zs_ref[rs, :]
    ys_scr[rs, :] = (_rms_scale(y) * nw).astype(BF16)


def _attend(g, p, q_ref, ktp_ref, ktc_ref, vp_ref, vc_ref, bias_ref, ya_scr, first_tile):
    ts = q_ref.shape[0]
    slab = ATTN_SLAB_HEADS * ATTN_HEAD_DIM
    lane_head = lax.broadcasted_iota(jnp.int32, (ATTN_GROUP_Q, slab), 1) >> HEAD_SHIFT
    lo = g * ATTN_GROUP_Q
    hi = lo + ATTN_GROUP_Q
    sl = slice(p * slab, (p + 1) * slab)
    kt = jnp.concatenate([ktp_ref[sl, lo:ts], ktc_ref[sl, 0:hi]], axis=1)
    vs = jnp.concatenate([vp_ref[lo:ts, sl], vc_ref[0:hi, sl]], axis=0)
    qg = q_ref[lo:hi, sl]
    zero = jnp.zeros_like(qg)
    qs = jnp.concatenate([jnp.where(lane_head == a, qg, zero) for a in range(ATTN_SLAB_HEADS)], axis=0)
    s = _dot(qs, kt) + bias_ref[p]
    if first_tile:
        band_pos = lax.broadcasted_iota(jnp.int32, (1, ATTN_GROUP_K), 1)
        s = jnp.where(band_pos >= ts - lo, s, jnp.finfo(F32).min)
    e = jnp.exp2(s - jnp.max(s, axis=-1, keepdims=True))
    denom = jnp.sum(e, axis=-1, keepdims=True)
    o = _dot(e.astype(BF16), vs) / denom
    og = o[:ATTN_GROUP_Q]
    for a in range(1, ATTN_SLAB_HEADS):
        og = jnp.where(lane_head == a, o[a * ATTN_GROUP_Q:(a + 1) * ATTN_GROUP_Q], og)
    ya_scr[lo:hi, sl] = og.astype(BF16)


def _mixer_kernel(xs_ref, bc_ref, dt_ref, zs_ref, alog_ref, dskip_ref, snw_ref,
                  q_ref, ktp_ref, ktc_ref, vp_ref, vc_ref, bias_ref, x_ref, mod_ref, ws_ref, wa_ref,
                  o_ref, state_scr, ys_scr, ya_scr):
    ts = xs_ref.shape[0]
    slabs = ATTN_WIDTH // (ATTN_SLAB_HEADS * ATTN_HEAD_DIM)
    attn_bodies = [(g, p) for g in range(ts // ATTN_GROUP_Q) for p in range(slabs)]
    ssd_chunks = ts // CHUNK
    assert len(attn_bodies) == ssd_chunks

    def token_mixing(first_tile):
        if first_tile:
            state_scr[...] = jnp.zeros_like(state_scr)
        a_neg = -jnp.exp(alog_ref[...])
        dskip = dskip_ref[...]
        snw = snw_ref[...]
        for c in range(ssd_chunks):
            _ssd_chunk(c, xs_ref, bc_ref, dt_ref, zs_ref, a_neg, dskip, snw, ys_scr, state_scr)
            _attend(*attn_bodies[c], q_ref, ktp_ref, ktc_ref, vp_ref, vc_ref, bias_ref, ya_scr, first_tile)

    first = pl.program_id(1) == 0
    pl.when(first)(lambda: token_mixing(True))
    pl.when(jnp.logical_not(first))(lambda: token_mixing(False))

    mix = _dot(ys_scr[...], ws_ref[...]) + _dot(ya_scr[...], wa_ref[...])
    o_ref[...] = x_ref[...] + mod_ref[0, 2:3, :] * mix


def _mixer_call(l, xs, bc, dt, zs, alog, dskip, snw, q, kt, v, bias, x, mod, wo, batch, seq):
    t, d = x.shape
    ts = SEQ_TILE
    assert ts == LEFT_CHUNKS * CHUNK
    nt = seq // ts
    tile = lambda w: pl.BlockSpec((ts, w), lambda b, i: (b * nt + i, 0))
    prev = pl.BlockSpec((ts, ATTN_WIDTH), lambda b, i: (jnp.maximum(b * nt + i - 1, 0), 0))
    cur_t = pl.BlockSpec((ATTN_WIDTH, ts), lambda b, i: (0, b * nt + i))
    prev_t = pl.BlockSpec((ATTN_WIDTH, ts), lambda b, i: (0, jnp.maximum(b * nt + i - 1, 0)))
    return pl.pallas_call(
        _mixer_kernel,
        grid=(batch, nt),
        in_specs=[tile(SSD_WIDTH), tile(XBC_DIM - SSD_WIDTH), tile(SSD_WIDTH), tile(SSD_WIDTH),
                  _layer(alog, l), _layer(dskip, l), _layer(snw, l),
                  tile(ATTN_WIDTH), prev_t, cur_t, prev, tile(ATTN_WIDTH), _layer(bias, l),
                  tile(d), pl.BlockSpec((None, 1, 6, d), lambda b, i: (l, b, 0, 0)),
                  _layer(wo, l, (SSD_WIDTH, d), (0, 0)), _layer(wo, l, (ATTN_WIDTH, d), (1, 0))],
        out_specs=tile(d),
        out_shape=jax.ShapeDtypeStruct((t, d), F32),
        scratch_shapes=[pltpu.VMEM((SSD_STATE, SSD_WIDTH), F32),
                        pltpu.VMEM((ts, SSD_WIDTH), BF16), pltpu.VMEM((ts, ATTN_WIDTH), BF16)],
        compiler_params=_params("arbitrary", "arbitrary"),
        name="mixer",
    )(xs, bc, dt, zs, alog, dskip, snw, q, kt, kt, v, v, bias, x, mod, wo, wo)


def _ffn_kernel(x_ref, mod_ref, nw_ref, wu_ref, cw_ref, wd_ref, fnw_ref, o_ref,
                h_scr, act_scr, *, tiles_per_seq, final):
    tm = x_ref.shape[0]
    i = pl.program_id(0)
    x = x_ref[...]
    h = _rms_scale(x) * nw_ref[...]
    h = h * (1.0 + mod_ref[0, 4:5, :]) + mod_ref[0, 3:4, :]

    @pl.when(i % tiles_per_seq == 0)
    def _():
        h_scr[0:HALO, :] = jnp.zeros((HALO, D_MODEL), BF16)

    h_scr[HALO:HALO + tm, :] = h.astype(BF16)

    for c in range(FFN_DIM // FFN_CHUNK):
        gate = slice(c * FFN_CHUNK, (c + 1) * FFN_CHUNK)
        value = slice(FFN_DIM + c * FFN_CHUNK, FFN_DIM + (c + 1) * FFN_CHUNK)
        pick = lambda ref, rows: jnp.concatenate([ref[rows, gate], ref[rows, value]], axis=1)
        ue = _dot(h_scr[...], pick(wu_ref, slice(None)))
        u = ue[HALO:HALO + tm, :] * pick(cw_ref, slice(FFN_CONV - 1, FFN_CONV))
        for k in range(FFN_CONV - 1):
            off = HALO - (FFN_CONV - 1) + k
            u = u + ue[off:off + tm, :] * pick(cw_ref, slice(k, k + 1))
        act_scr[:, gate] = (_silu(u[:, :FFN_CHUNK]) * u[:, FFN_CHUNK:]).astype(BF16)

    h_scr[0:HALO, :] = h_scr[tm:tm + HALO, :]
    out = x + mod_ref[0, 5:6, :] * _dot(act_scr[...], wd_ref[...])
    if final:
        out = _rms_scale(out) * fnw_ref[...]
    o_ref[...] = out


def _ffn_call(l, x, mod, nw, wu, cw, wd, fnw, seq, final):
    t, d = x.shape
    tm = ROW_TILE
    tiles_per_seq = seq // tm
    row = pl.BlockSpec((tm, d), lambda i: (i, 0))
    return pl.pallas_call(
        functools.partial(_ffn_kernel, tiles_per_seq=tiles_per_seq, final=final),
        grid=(t // tm,),
        in_specs=[row, pl.BlockSpec((None, 1, 6, d), lambda i: (l, i // tiles_per_seq, 0, 0)), _layer(nw, l),
                  _layer(wu, l), _layer(cw, l), _layer(wd, l), _resident((1, d))],
        out_specs=row,
        out_shape=jax.ShapeDtypeStruct((t, d), F32),
        scratch_shapes=[
            pltpu.VMEM((tm + HALO, d), BF16),
            pltpu.VMEM((tm, FFN_DIM), BF16),
        ],
        compiler_params=_params("arbitrary"),
        name="conv_ffn",
    )(x, mod, nw, wu, cw, wd, fnw)


def _rel_bias_table(rel_bias):
    depth, heads = rel_bias.shape[:2]
    rb = rel_bias.astype(F32).reshape(depth * heads, -1) * LOG2E
    period = ATTN_GROUP_Q + ATTN_GROUP_K
    far = rb[:, 2 * MAX_REL:]
    flat_len = LEFT_CHUNKS * CHUNK - MAX_REL
    unseen = period - flat_len - 2 * MAX_REL
    w = jnp.concatenate([jnp.repeat(far, flat_len, axis=1), rb[:, :0:-1], jnp.repeat(far, unseen, axis=1)], axis=1)
    assert w.shape[1] == period
    toep = jnp.tile(w, (1, ATTN_GROUP_Q))[:, :ATTN_GROUP_Q * (period - 1)]
    toep = toep.reshape(depth * heads, ATTN_GROUP_Q, period - 1)[:, :, :ATTN_GROUP_K]
    first = (np.arange(ATTN_GROUP_Q)[:, None] // CHUNK) * CHUNK
    pos = np.arange(ATTN_GROUP_K)[None, :]
    window = (pos >= first) & (pos < first + BAND)
    table = jnp.where(window[None], toep, -1e30)
    return table.reshape(depth, heads // ATTN_SLAB_HEADS, ATTN_SLAB_HEADS * ATTN_GROUP_Q, ATTN_GROUP_K)


def _per_lane(v):
    return jnp.repeat(v.astype(F32), SSD_HEAD_DIM, axis=1)[:, None, :]


def kernel(x, c, norm_mix_w, w_ada, b_ada, w_in, ssd_conv_w, ssd_conv_b, dt_bias, a_log, d_skip, ssd_norm_w,
           rel_bias, w_out, norm_ffn_w, w_up, ffn_conv_w, w_down, final_norm_w):
    batch, seq, d = x.shape
    depth = w_ada.shape[0]
    assert d == D_MODEL and seq % SEQ_TILE == 0 and seq % ROW_TILE == 0
    mod = _ada_call(c, w_ada, b_ada).reshape(depth, batch, 6, d)

    o_xbc, o_dt, o_q = SSD_WIDTH, SSD_WIDTH + XBC_DIM, SSD_WIDTH + XBC_DIM + SSD_HEADS
    o_k, o_v = o_q + ATTN_WIDTH, o_q + 2 * ATTN_WIDTH
    w_in_b = w_in.astype(BF16)
    wz, wxbc, wq, wv = w_in_b[:, :, :o_xbc], w_in_b[:, :, o_xbc:o_dt], w_in_b[:, :, o_q:o_k], w_in_b[:, :, o_v:]
    wdt = jnp.repeat(w_in_b[:, :, o_dt:o_q], SSD_HEAD_DIM, axis=2)
    wkt = jnp.swapaxes(w_in_b[:, :, o_k:o_v], 1, 2)
    wo, wu, wd = w_out.astype(BF16), w_up.astype(BF16), w_down.astype(BF16)
    row3 = lambda a: a[:, None, :]
    nmw, nfw, cb, snw = row3(norm_mix_w), row3(norm_ffn_w), row3(ssd_conv_b), row3(ssd_norm_w)
    dtb, alog, dskip = _per_lane(dt_bias), _per_lane(a_log), _per_lane(d_skip)
    bias = _rel_bias_table(rel_bias)

    xf = x.reshape(batch * seq, d)
    for l in range(depth):
        zs, xs, bc, dt, q, kt, v = _inproj_call(l, xf, mod, nmw, wz, wxbc, wdt, wq, wkt, wv, ssd_conv_w, cb, dtb, seq)
        xf = _mixer_call(l, xs, bc, dt, zs, alog, dskip, snw, q, kt, v, bias, xf, mod, wo, batch, seq)
        xf = _ffn_call(l, xf, mod, nfw, wu, ffn_conv_w, wd, final_norm_w[None, :], seq, final=(l == depth - 1))
    return xf.reshape(batch, seq, d)
```

```python
import functools
import math

import numpy as np
import jax
import jax.numpy as jnp
from jax import lax
from jax.experimental import pallas as pl
from jax.experimental.pallas import tpu as pltpu

F32 = jnp.float32
BF16 = jnp.bfloat16

D_MODEL = 1024
CHUNK = 64
SSD_WIDTH = 512
SSD_HEAD_DIM = 64
SSD_HEADS = 8
SSD_GROUPS = 2
SSD_STATE = 128
SSD_CONV = 4
XBC_DIM = SSD_WIDTH + 2 * SSD_GROUPS * SSD_STATE
GROUP_WIDTH = SSD_WIDTH // SSD_GROUPS
ATTN_WIDTH = 512
ATTN_HEAD_DIM = 64
ATTN_HEADS = 8
LEFT_CHUNKS = 8
BAND = (LEFT_CHUNKS + 1) * CHUNK
MAX_REL = 128
FFN_DIM = 2816
FFN_CONV = 3
EPS = 1e-6
LOG2E = math.log2(math.e)

HEAD_SHIFT = 6
LANES = 128
SUBLANES = 8
ROW_TILE = 1024
SEQ_TILE = 512
ATTN_GROUP_Q = 2 * CHUNK
ATTN_GROUP_K = ATTN_GROUP_Q + LEFT_CHUNKS * CHUNK
ATTN_SLAB_HEADS = 4
FFN_CHUNK = 256
HALO = 16
VMEM_LIMIT = 56 * 1024 * 1024

NT_DIMS = (((1,), (1,)), ((), ()))
TN_DIMS = (((0,), (0,)), ((), ()))


def _dot(a, b):
    return jnp.dot(a, b, preferred_element_type=F32)


def _silu(x):
    half = 0.5 * x
    return half + half * jnp.tanh(half)


def _softplus(x):
    return jnp.maximum(x, 0.0) + jnp.log(1.0 + jnp.exp(-jnp.abs(x)))


def _rms_scale(x):
    return x * lax.rsqrt(jnp.mean(x * x, axis=-1, keepdims=True) + EPS)


def _params(*semantics):
    return pltpu.CompilerParams(dimension_semantics=semantics, vmem_limit_bytes=VMEM_LIMIT)


def _resident(shape):
    return pl.BlockSpec(shape, lambda *_: (0,) * len(shape), pipeline_mode=pl.Buffered(1))


def _layer(arr, l, block=None, index=None):
    block = tuple(arr.shape[1:]) if block is None else block
    index = (0,) * len(block) if index is None else index
    return pl.BlockSpec((None,) + block, lambda *_: (l,) + index, pipeline_mode=pl.Buffered(1))


def _ada_kernel(c_ref, w_ref, b_ref, o_ref):
    c = c_ref[...]
    c_act = _silu(c).astype(BF16)
    o_ref[0] = _dot(c_act, w_ref[0].astype(BF16)) + b_ref[0]


def _ada_call(c, w_ada, b_ada):
    depth, d, n = w_ada.shape
    b = c.shape[0]
    tn = n // 4
    return pl.pallas_call(
        _ada_kernel,
        grid=(depth, n // tn),
        in_specs=[
            pl.BlockSpec((b, d), lambda l, j: (0, 0)),
            pl.BlockSpec((1, d, tn), lambda l, j: (l, 0, j)),
            pl.BlockSpec((1, 1, tn), lambda l, j: (l, 0, j)),
        ],
        out_specs=pl.BlockSpec((1, b, tn), lambda l, j: (l, 0, j)),
        out_shape=jax.ShapeDtypeStruct((depth, b, n), F32),
        compiler_params=_params("arbitrary", "arbitrary"),
        name="ada_mod",
    )(c, w_ada, b_ada.reshape(depth, 1, n))


def _inproj_kernel(x_ref, mod_ref, nw_ref, wz_ref, wxbc_ref, wdt_ref, wq_ref, wkt_ref, wv_ref,
                   cw_ref, cb_ref, dtb_ref,
                   zs_ref, xs_ref, bc_ref, dt_ref, q_ref, kt_ref, v_ref, conv_scr, *, tiles_per_seq):
    tm = x_ref.shape[0]
    h = _rms_scale(x_ref[...]) * nw_ref[...]
    h = h * (1.0 + mod_ref[0, 1:2, :]) + mod_ref[0, 0:1, :]
    hb = h.astype(BF16)

    zs_ref[...] = _silu(_dot(hb, wz_ref[...]))

    @pl.when(pl.program_id(0) % tiles_per_seq == 0)
    def _():
        conv_scr[...] = jnp.zeros((SUBLANES, XBC_DIM), F32)

    xbc = _dot(hb, wxbc_ref[...])
    ext = jnp.concatenate([conv_scr[...], xbc], axis=0)
    acc = cb_ref[...] + xbc * cw_ref[SSD_CONV - 1:SSD_CONV, :]
    for k in range(SSD_CONV - 1):
        off = SUBLANES - (SSD_CONV - 1) + k
        acc = acc + ext[off:off + tm, :] * cw_ref[k:k + 1, :]
    xc = _silu(acc)
    xs_ref[...] = xc[:, :SSD_WIDTH]
    bc_ref[...] = xc[:, SSD_WIDTH:].astype(BF16)
    conv_scr[...] = xbc[tm - SUBLANES:tm, :]

    dt_ref[...] = _softplus(_dot(hb, wdt_ref[...]) + dtb_ref[...])
    q_ref[...] = (_dot(hb, wq_ref[...]) * (LOG2E * ATTN_HEAD_DIM ** -0.5)).astype(BF16)
    kt_ref[...] = lax.dot_general(wkt_ref[...], hb, NT_DIMS, preferred_element_type=F32).astype(BF16)
    v_ref[...] = _dot(hb, wv_ref[...]).astype(BF16)


def _inproj_call(l, x, mod, nw, wz, wxbc, wdt, wq, wkt, wv, cw, cb, dtb, seq):
    t, d = x.shape
    tm = ROW_TILE
    tiles_per_seq = seq // tm
    row = lambda w: pl.BlockSpec((tm, w), lambda i: (i, 0))
    col = pl.BlockSpec((ATTN_WIDTH, tm), lambda i: (0, i))
    sds = jax.ShapeDtypeStruct
    return pl.pallas_call(
        functools.partial(_inproj_kernel, tiles_per_seq=tiles_per_seq),
        grid=(t // tm,),
        in_specs=[
            row(d),
            pl.BlockSpec((None, 1, 6, d), lambda i: (l, i // tiles_per_seq, 0, 0)),
            _layer(nw, l),
            _layer(wz, l), _layer(wxbc, l), _layer(wdt, l), _layer(wq, l), _layer(wkt, l), _layer(wv, l),
            _layer(cw, l), _layer(cb, l), _layer(dtb, l),
        ],
        out_specs=[row(SSD_WIDTH), row(SSD_WIDTH), row(XBC_DIM - SSD_WIDTH), row(SSD_WIDTH),
                   row(ATTN_WIDTH), col, row(ATTN_WIDTH)],
        out_shape=[sds((t, SSD_WIDTH), F32), sds((t, SSD_WIDTH), F32), sds((t, XBC_DIM - SSD_WIDTH), BF16),
                   sds((t, SSD_WIDTH), F32),
                   sds((t, ATTN_WIDTH), BF16), sds((ATTN_WIDTH, t), BF16), sds((t, ATTN_WIDTH), BF16)],
        scratch_shapes=[pltpu.VMEM((SUBLANES, XBC_DIM), F32)],
        compiler_params=_params("arbitrary"),
        name="in_proj",
    )(x, mod, nw, wz, wxbc, wdt, wq, wkt, wv, cw, cb, dtb)


def _split3(a):
    hi = a.astype(BF16)
    r1 = a - hi.astype(F32)
    mid = r1.astype(BF16)
    lo = (r1 - mid.astype(F32)).astype(BF16)
    return hi, mid, lo


def _ssd_chunk(c, xs_ref, bc_ref, dt_ref, zs_ref, a_neg, dskip, nw, ys_scr, state_scr):
    rows = lax.broadcasted_iota(jnp.int32, (CHUNK, CHUNK), 0)
    cols = lax.broadcasted_iota(jnp.int32, (CHUNK, CHUNK), 1)
    tri = (rows >= cols).astype(BF16)
    row_w = lax.broadcasted_iota(jnp.int32, (CHUNK, SSD_WIDTH), 0)
    lane_w = lax.broadcasted_iota(jnp.int32, (CHUNK, SSD_WIDTH), 1)
    pos_w = lane_w & (SSD_HEAD_DIM - 1)
    diag_w = row_w == pos_w
    causal_w = row_w >= pos_w
    lane_g = lax.broadcasted_iota(jnp.int32, (CHUNK, GROUP_WIDTH), 1)
    heads_per_group = SSD_HEADS // SSD_GROUPS

    rs = slice(c * CHUNK, (c + 1) * CHUNK)
    xs = xs_ref[rs, :]
    dt = dt_ref[rs, :]
    adt = dt * a_neg
    xdt = xs * dt
    a_cs = sum(_dot(tri, piece) for piece in _split3(adt))
    a_last = a_cs[CHUNK - 1:CHUNK, :]
    a_row = jnp.sum(jnp.where(diag_w, a_cs, 0.0), axis=0, keepdims=True)
    decay = jnp.exp2(jnp.where(causal_w, a_cs - a_row, -jnp.inf))
    decay_in = jnp.exp2(a_cs)
    decay_out = jnp.exp2(a_last - a_cs)
    chunk_decay = jnp.exp2(a_last)

    ys = []
    for g in range(SSD_GROUPS):
        gl = slice(g * GROUP_WIDTH, (g + 1) * GROUP_WIDTH)
        b_gb = bc_ref[rs, g * SSD_STATE:(g + 1) * SSD_STATE]
        c_gb = bc_ref[rs, (SSD_GROUPS + g) * SSD_STATE:(SSD_GROUPS + g + 1) * SSD_STATE]
        cb_rep = lax.dot_general(c_gb, jnp.concatenate([b_gb] * heads_per_group, axis=0), NT_DIMS,
                                 preferred_element_type=F32)
        m = (cb_rep * decay[:, gl]).astype(BF16)
        x_gb = xdt[:, gl].astype(BF16)
        x_blockdiag = jnp.concatenate(
            [jnp.where(lane_g >> HEAD_SHIFT == j, x_gb, jnp.zeros_like(x_gb)) for j in range(heads_per_group)],
            axis=0)
        y_diag = _dot(m, x_blockdiag)
        prev = state_scr[:, gl]
        y_off = _dot(c_gb, prev.astype(BF16)) * decay_in[:, gl]
        xd = (xdt[:, gl] * decay_out[:, gl]).astype(BF16)
        new_states = lax.dot_general(b_gb, xd, TN_DIMS, preferred_element_type=F32)
        state_scr[:, gl] = prev * chunk_decay[:, gl] + new_states
        ys.append(y_diag + y_off)
    y = (jnp.concatenate(ys, axis=1) + dskip * xs) * zs_ref[rs, :]
    ys_scr[rs, :] = (_rms_scale(y) * nw).astype(BF16)


def _ssd_kernel(xs_ref, bc_ref, dt_ref, zs_ref, alog_ref, dskip_ref, nw_ref, y_ref, state_scr):
    @pl.when(pl.program_id(1) == 0)
    def _():
        state_scr[...] = jnp.zeros_like(state_scr)

    a_neg = -jnp.exp(alog_ref[...]) * LOG2E
    dskip = dskip_ref[...]
    nw = nw_ref[...]
    for c in range(xs_ref.shape[0] // CHUNK):
        _ssd_chunk(c, xs_ref, bc_ref, dt_ref, zs_ref, a_neg, dskip, nw, y_ref, state_scr)


def _ssd_call(l, xs, bc, dt, zs, alog, dskip, nw, batch, seq):
    t = xs.shape[0]
    ts = SEQ_TILE
    nt = seq // ts
    tile = lambda w: pl.BlockSpec((ts, w), lambda b, i: (b * nt + i, 0))
    return pl.pallas_call(
        _ssd_kernel,
        grid=(batch, nt),
        in_specs=[tile(SSD_WIDTH), tile(XBC_DIM - SSD_WIDTH), tile(SSD_WIDTH), tile(SSD_WIDTH),
                  _layer(alog, l), _layer(dskip, l), _layer(nw, l)],
        out_specs=tile(SSD_WIDTH),
        out_shape=jax.ShapeDtypeStruct((t, SSD_WIDTH), BF16),
        scratch_shapes=[pltpu.VMEM((SSD_STATE, SSD_WIDTH), F32)],
        compiler_params=_params("arbitrary", "arbitrary"),
        name="ssd_scan",
    )(xs, bc, dt, zs, alog, dskip, nw)


def _attend(g, p, q_ref, ktp_ref, ktc_ref, vp_ref, vc_ref, bias_ref, ya_scr, first_tile):
    ts = q_ref.shape[0]
    slab = ATTN_SLAB_HEADS * ATTN_HEAD_DIM
    lane_head = lax.broadcasted_iota(jnp.int32, (ATTN_GROUP_Q, slab), 1) >> HEAD_SHIFT
    lo = g * ATTN_GROUP_Q
    hi = lo + ATTN_GROUP_Q
    sl = slice(p * slab, (p + 1) * slab)
    kt = jnp.concatenate([ktp_ref[sl, lo:ts], ktc_ref[sl, 0:hi]], axis=1)
    vs = jnp.concatenate([vp_ref[lo:ts, sl], vc_ref[0:hi, sl]], axis=0)
    qg = q_ref[lo:hi, sl]
    zero = jnp.zeros_like(qg)
    qs = jnp.concatenate([jnp.where(lane_head == a, qg, zero) for a in range(ATTN_SLAB_HEADS)], axis=0)
    s = _dot(qs, kt) + bias_ref[p]
    if first_tile:
        band_pos = lax.broadcasted_iota(jnp.int32, (1, ATTN_GROUP_K), 1)
        s = jnp.where(band_pos >= ts - lo, s, jnp.finfo(F32).min)
    e = jnp.exp2(s - jnp.max(s, axis=-1, keepdims=True))
    denom = jnp.sum(e, axis=-1, keepdims=True)
    o = _dot(e.astype(BF16), vs) / denom
    og = o[:ATTN_GROUP_Q]
    for a in range(1, ATTN_SLAB_HEADS):
        og = jnp.where(lane_head == a, o[a * ATTN_GROUP_Q:(a + 1) * ATTN_GROUP_Q], og)
    ya_scr[lo:hi, sl] = og.astype(BF16)


def _attn_kernel(q_ref, ktp_ref, ktc_ref, vp_ref, vc_ref, bias_ref, ys_ref, x_ref, mod_ref, ws_ref, wa_ref,
                 o_ref, ya_scr):
    ts = q_ref.shape[0]
    slabs = ATTN_WIDTH // (ATTN_SLAB_HEADS * ATTN_HEAD_DIM)

    def attention(first_tile):
        for g in range(ts // ATTN_GROUP_Q):
            for p in range(slabs):
                _attend(g, p, q_ref, ktp_ref, ktc_ref, vp_ref, vc_ref, bias_ref, ya_scr, first_tile)

    first = pl.program_id(1) == 0
    pl.when(first)(lambda: attention(True))
    pl.when(jnp.logical_not(first))(lambda: attention(False))

    mix = _dot(ys_ref[...], ws_ref[...]) + _dot(ya_scr[...], wa_ref[...])
    o_ref[...] = x_ref[...] + mod_ref[0, 2:3, :] * mix


def _attn_call(l, q, kt, v, bias, ys, x, mod, wo, batch, seq):
    t, d = x.shape
    ts = SEQ_TILE
    assert ts == LEFT_CHUNKS * CHUNK
    nt = seq // ts
    tile = lambda w: pl.BlockSpec((ts, w), lambda b, i: (b * nt + i, 0))
    prev = pl.BlockSpec((ts, ATTN_WIDTH), lambda b, i: (jnp.maximum(b * nt + i - 1, 0), 0))
    cur_t = pl.BlockSpec((ATTN_WIDTH, ts), lambda b, i: (0, b * nt + i))
    prev_t = pl.BlockSpec((ATTN_WIDTH, ts), lambda b, i: (0, jnp.maximum(b * nt + i - 1, 0)))
    return pl.pallas_call(
        _attn_kernel,
        grid=(batch, nt),
        in_specs=[tile(ATTN_WIDTH), prev_t, cur_t, prev, tile(ATTN_WIDTH), _layer(bias, l),
                  tile(SSD_WIDTH), tile(d), pl.BlockSpec((None, 1, 6, d), lambda b, i: (l, b, 0, 0)),
                  _layer(wo, l, (SSD_WIDTH, d), (0, 0)), _layer(wo, l, (ATTN_WIDTH, d), (1, 0))],
        out_specs=tile(d),
        out_shape=jax.ShapeDtypeStruct((t, d), F32),
        scratch_shapes=[pltpu.VMEM((ts, ATTN_WIDTH), BF16)],
        compiler_params=_params("arbitrary", "arbitrary"),
        name="attn_out",
    )(q, kt, kt, v, v, bias, ys, x, mod, wo, wo)


def _ffn_kernel(x_ref, mod_ref, nw_ref, wu_ref, cw_ref, wd_ref, fnw_ref, o_ref,
                h_scr, u_scr, act_scr, *, tiles_per_seq, final):
    tm = x_ref.shape[0]
    i = pl.program_id(0)
    x = x_ref[...]
    h = _rms_scale(x) * nw_ref[...]
    h = h * (1.0 + mod_ref[0, 4:5, :]) + mod_ref[0, 3:4, :]

    @pl.when(i % tiles_per_seq == 0)
    def _():
        h_scr[0:HALO, :] = jnp.zeros((HALO, D_MODEL), BF16)

    h_scr[HALO:HALO + tm, :] = h.astype(BF16)

    for c in range(FFN_DIM // FFN_CHUNK):
        gate = slice(c * FFN_CHUNK, (c + 1) * FFN_CHUNK)
        value = slice(FFN_DIM + c * FFN_CHUNK, FFN_DIM + (c + 1) * FFN_CHUNK)
        pick = lambda ref, rows: jnp.concatenate([ref[rows, gate], ref[rows, value]], axis=1)
        buf = u_scr.at[c % 2]
        buf[...] = _dot(h_scr[...], pick(wu_ref, slice(None)))
        u = buf[HALO:HALO + tm, :] * pick(cw_ref, slice(FFN_CONV - 1, FFN_CONV))
        for k in range(FFN_CONV - 1):
            off = HALO - (FFN_CONV - 1) + k
            u = u + buf[off:off + tm, :] * pick(cw_ref, slice(k, k + 1))
        act_scr[:, gate] = (_silu(u[:, :FFN_CHUNK]) * u[:, FFN_CHUNK:]).astype(BF16)

    h_scr[0:HALO, :] = h_scr[tm:tm + HALO, :]
    out = x + mod_ref[0, 5:6, :] * _dot(act_scr[...], wd_ref[...])
    if final:
        out = _rms_scale(out) * fnw_ref[...]
    o_ref[...] = out


def _ffn_call(l, x, mod, nw, wu, cw, wd, fnw, seq, final):
    t, d = x.shape
    tm = ROW_TILE
    tiles_per_seq = seq // tm
    row = pl.BlockSpec((tm, d), lambda i: (i, 0))
    return pl.pallas_call(
        functools.partial(_ffn_kernel, tiles_per_seq=tiles_per_seq, final=final),
        grid=(t // tm,),
        in_specs=[row, pl.BlockSpec((None, 1, 6, d), lambda i: (l, i // tiles_per_seq, 0, 0)), _layer(nw, l),
                  _layer(wu, l), _layer(cw, l), _layer(wd, l), _resident((1, d))],
        out_specs=row,
        out_shape=jax.ShapeDtypeStruct((t, d), F32),
        scratch_shapes=[
            pltpu.VMEM((tm + HALO, d), BF16),
            pltpu.VMEM((2, tm + HALO, 2 * FFN_CHUNK), F32),
            pltpu.VMEM((tm, FFN_DIM), BF16),
        ],
        compiler_params=_params("arbitrary"),
        name="conv_ffn",
    )(x, mod, nw, wu, cw, wd, fnw)


def _rel_bias_table(rel_bias):
    depth, heads = rel_bias.shape[:2]
    rb = rel_bias.astype(F32).reshape(depth * heads, -1) * LOG2E
    period = ATTN_GROUP_Q + ATTN_GROUP_K
    far = rb[:, 2 * MAX_REL:]
    flat_len = LEFT_CHUNKS * CHUNK - MAX_REL
    unseen = period - flat_len - 2 * MAX_REL
    w = jnp.concatenate([jnp.repeat(far, flat_len, axis=1), rb[:, :0:-1], jnp.repeat(far, unseen, axis=1)], axis=1)
    assert w.shape[1] == period
    toep = jnp.tile(w, (1, ATTN_GROUP_Q))[:, :ATTN_GROUP_Q * (period - 1)]
    toep = toep.reshape(depth * heads, ATTN_GROUP_Q, period - 1)[:, :, :ATTN_GROUP_K]
    first = (np.arange(ATTN_GROUP_Q)[:, None] // CHUNK) * CHUNK
    pos = np.arange(ATTN_GROUP_K)[None, :]
    window = (pos >= first) & (pos < first + BAND)
    table = jnp.where(window[None], toep, -1e30)
    return table.reshape(depth, heads // ATTN_SLAB_HEADS, ATTN_SLAB_HEADS * ATTN_GROUP_Q, ATTN_GROUP_K)


def _per_lane(v):
    return jnp.repeat(v.astype(F32), SSD_HEAD_DIM, axis=1)[:, None, :]


def kernel(x, c, norm_mix_w, w_ada, b_ada, w_in, ssd_conv_w, ssd_conv_b, dt_bias, a_log, d_skip, ssd_norm_w,
           rel_bias, w_out, norm_ffn_w, w_up, ffn_conv_w, w_down, final_norm_w):
    batch, seq, d = x.shape
    depth = w_ada.shape[0]
    assert d == D_MODEL and seq % SEQ_TILE == 0 and seq % ROW_TILE == 0
    mod = _ada_call(c, w_ada, b_ada).reshape(depth, batch, 6, d)

    o_xbc, o_dt, o_q = SSD_WIDTH, SSD_WIDTH + XBC_DIM, SSD_WIDTH + XBC_DIM + SSD_HEADS
    o_k, o_v = o_q + ATTN_WIDTH, o_q + 2 * ATTN_WIDTH
    cols = lambda lo, hi: w_in[:, :, lo:hi].astype(BF16)
    wz, wxbc, wq, wv = cols(0, o_xbc), cols(o_xbc, o_dt), cols(o_q, o_k), cols(o_v, None)
    wdt = jnp.repeat(cols(o_dt, o_q), SSD_HEAD_DIM, axis=2)
    wkt = jnp.swapaxes(cols(o_k, o_v), 1, 2)
    wo, wu, wd = w_out.astype(BF16), w_up.astype(BF16), w_down.astype(BF16)
    row3 = lambda a: a[:, None, :]
    nmw, nfw, cb, snw = row3(norm_mix_w), row3(norm_ffn_w), row3(ssd_conv_b), row3(ssd_norm_w)
    dtb, alog, dskip = _per_lane(dt_bias), _per_lane(a_log), _per_lane(d_skip)
    bias = _rel_bias_table(rel_bias)

    xf = x.reshape(batch * seq, d)
    for l in range(depth):
        zs, xs, bc, dt, q, kt, v = _inproj_call(l, xf, mod, nmw, wz, wxbc, wdt, wq, wkt, wv, ssd_conv_w, cb, dtb, seq)
        y_ssd = _ssd_call(l, xs, bc, dt, zs, alog, dskip, snw, batch, seq)
        xf = _attn_call(l, q, kt, v, bias, y_ssd, xf, mod, wo, batch, seq)
        xf = _ffn_call(l, xf, mod, nfw, wu, ffn_conv_w, wd, final_norm_w[None, :], seq, final=(l == depth - 1))
    return xf.reshape(batch, seq, d)
```

```python
import functools
import math

import numpy as np
import jax
import jax.numpy as jnp
from jax import lax
from jax.experimental import pallas as pl
from jax.experimental.pallas import tpu as pltpu

F32 = jnp.float32
BF16 = jnp.bfloat16

D_MODEL = 1024
CHUNK = 64
SSD_WIDTH = 512
SSD_HEAD_DIM = 64
SSD_HEADS = 8
SSD_GROUPS = 2
SSD_STATE = 128
SSD_CONV = 4
XBC_DIM = SSD_WIDTH + 2 * SSD_GROUPS * SSD_STATE
GROUP_WIDTH = SSD_WIDTH // SSD_GROUPS
ATTN_WIDTH = 512
ATTN_HEAD_DIM = 64
ATTN_HEADS = 8
LEFT_CHUNKS = 8
BAND = (LEFT_CHUNKS + 1) * CHUNK
MAX_REL = 128
FFN_DIM = 2816
FFN_CONV = 3
EPS = 1e-6
LOG2E = math.log2(math.e)

HEAD_SHIFT = 6
LANES = 128
SUBLANES = 8
ROW_TILE = 1024
SEQ_TILE = 1024
ATTN_GROUP_Q = 2 * CHUNK
ATTN_HIST = LEFT_CHUNKS * CHUNK
ATTN_GROUP_K = ATTN_GROUP_Q + ATTN_HIST
ATTN_SLAB_HEADS = 4
FFN_CHUNK = 256
HALO = 16
VMEM_LIMIT = 56 * 1024 * 1024

NT_DIMS = (((1,), (1,)), ((), ()))
TN_DIMS = (((0,), (0,)), ((), ()))


def _dot(a, b):
    return jnp.dot(a, b, preferred_element_type=F32)


def _silu(x):
    half = 0.5 * x
    return half + half * jnp.tanh(half)


def _softplus(x):
    return jnp.maximum(x, 0.0) + jnp.log(1.0 + jnp.exp(-jnp.abs(x)))


def _rms_scale(x):
    return x * lax.rsqrt(jnp.mean(x * x, axis=-1, keepdims=True) + EPS)


def _params(*semantics):
    return pltpu.CompilerParams(dimension_semantics=semantics, vmem_limit_bytes=VMEM_LIMIT)


def _resident(shape):
    return pl.BlockSpec(shape, lambda *_: (0,) * len(shape), pipeline_mode=pl.Buffered(1))


def _layer(arr, l, block=None, index=None):
    block = tuple(arr.shape[1:]) if block is None else block
    index = (0,) * len(block) if index is None else index
    return pl.BlockSpec((None,) + block, lambda *_: (l,) + index, pipeline_mode=pl.Buffered(1))


def _ada_kernel(c_ref, w_ref, b_ref, o_ref):
    c = c_ref[...]
    c_act = _silu(c).astype(BF16)
    o_ref[0] = _dot(c_act, w_ref[0].astype(BF16)) + b_ref[0]


def _ada_call(c, w_ada, b_ada):
    depth, d, n = w_ada.shape
    b = c.shape[0]
    tn = n // 4
    return pl.pallas_call(
        _ada_kernel,
        grid=(depth, n // tn),
        in_specs=[
            pl.BlockSpec((b, d), lambda l, j: (0, 0)),
            pl.BlockSpec((1, d, tn), lambda l, j: (l, 0, j)),
            pl.BlockSpec((1, 1, tn), lambda l, j: (l, 0, j)),
        ],
        out_specs=pl.BlockSpec((1, b, tn), lambda l, j: (l, 0, j)),
        out_shape=jax.ShapeDtypeStruct((depth, b, n), F32),
        compiler_params=_params("arbitrary", "arbitrary"),
        name="ada_mod",
    )(c, w_ada, b_ada.reshape(depth, 1, n))


def _inproj_kernel(x_ref, mod_ref, nw_ref, wz_ref, wxbc_ref, wdt_ref, wq_ref, wkt_ref, wv_ref,
                   cw_ref, cb_ref, dtb_ref,
                   zs_ref, xs_ref, bc_ref, dt_ref, q_ref, kt_ref, v_ref, conv_scr, *, tiles_per_seq):
    tm = x_ref.shape[0]
    h = _rms_scale(x_ref[...]) * nw_ref[...]
    h = h * (1.0 + mod_ref[0, 1:2, :]) + mod_ref[0, 0:1, :]
    hb = h.astype(BF16)

    zs_ref[...] = _silu(_dot(hb, wz_ref[...]))

    @pl.when(pl.program_id(0) % tiles_per_seq == 0)
    def _():
        conv_scr[...] = jnp.zeros((SUBLANES, XBC_DIM), F32)

    xbc = _dot(hb, wxbc_ref[...])
    ext = jnp.concatenate([conv_scr[...], xbc], axis=0)
    acc = cb_ref[...] + xbc * cw_ref[SSD_CONV - 1:SSD_CONV, :]
    for k in range(SSD_CONV - 1):
        off = SUBLANES - (SSD_CONV - 1) + k
        acc = acc + ext[off:off + tm, :] * cw_ref[k:k + 1, :]
    xc = _silu(acc)
    xs_ref[...] = xc[:, :SSD_WIDTH]
    bc_ref[...] = xc[:, SSD_WIDTH:].astype(BF16)
    conv_scr[...] = xbc[tm - SUBLANES:tm, :]

    dt_ref[...] = _softplus(_dot(hb, wdt_ref[...]) + dtb_ref[...])
    q_ref[...] = (_dot(hb, wq_ref[...]) * (LOG2E * ATTN_HEAD_DIM ** -0.5)).astype(BF16)
    kt_ref[...] = lax.dot_general(wkt_ref[...], hb, NT_DIMS, preferred_element_type=F32).astype(BF16)
    v_ref[...] = _dot(hb, wv_ref[...]).astype(BF16)


def _inproj_call(l, x, mod, nw, wz, wxbc, wdt, wq, wkt, wv, cw, cb, dtb, seq):
    t, d = x.shape
    tm = ROW_TILE
    tiles_per_seq = seq // tm
    row = lambda w: pl.BlockSpec((tm, w), lambda i: (i, 0))
    col = pl.BlockSpec((ATTN_WIDTH, tm), lambda i: (0, i))
    sds = jax.ShapeDtypeStruct
    return pl.pallas_call(
        functools.partial(_inproj_kernel, tiles_per_seq=tiles_per_seq),
        grid=(t // tm,),
        in_specs=[
            row(d),
            pl.BlockSpec((None, 1, 6, d), lambda i: (l, i // tiles_per_seq, 0, 0)),
            _layer(nw, l),
            _layer(wz, l), _layer(wxbc, l), _layer(wdt, l), _layer(wq, l), _layer(wkt, l), _layer(wv, l),
            _layer(cw, l), _layer(cb, l), _layer(dtb, l),
        ],
        out_specs=[row(SSD_WIDTH), row(SSD_WIDTH), row(XBC_DIM - SSD_WIDTH), row(SSD_WIDTH),
                   row(ATTN_WIDTH), col, row(ATTN_WIDTH)],
        out_shape=[sds((t, SSD_WIDTH), F32), sds((t, SSD_WIDTH), F32), sds((t, XBC_DIM - SSD_WIDTH), BF16),
                   sds((t, SSD_WIDTH), F32),
                   sds((t, ATTN_WIDTH), BF16), sds((ATTN_WIDTH, t), BF16), sds((t, ATTN_WIDTH), BF16)],
        scratch_shapes=[pltpu.VMEM((SUBLANES, XBC_DIM), F32)],
        compiler_params=_params("arbitrary"),
        name="in_proj",
    )(x, mod, nw, wz, wxbc, wdt, wq, wkt, wv, cw, cb, dtb)


def _split3(a):
    hi = a.astype(BF16)
    r1 = a - hi.astype(F32)
    mid = r1.astype(BF16)
    lo = (r1 - mid.astype(F32)).astype(BF16)
    return hi, mid, lo


def _ssd_chunk(c, xs_ref, bc_ref, dt_ref, zs_ref, a_neg, dskip, nw, ys_scr, state_scr):
    rows = lax.broadcasted_iota(jnp.int32, (CHUNK, CHUNK), 0)
    cols = lax.broadcasted_iota(jnp.int32, (CHUNK, CHUNK), 1)
    tri = (rows >= cols).astype(BF16)
    row_w = lax.broadcasted_iota(jnp.int32, (CHUNK, SSD_WIDTH), 0)
    lane_w = lax.broadcasted_iota(jnp.int32, (CHUNK, SSD_WIDTH), 1)
    pos_w = lane_w & (SSD_HEAD_DIM - 1)
    diag_w = row_w == pos_w
    causal_w = row_w >= pos_w
    lane_g = lax.broadcasted_iota(jnp.int32, (CHUNK, GROUP_WIDTH), 1)
    heads_per_group = SSD_HEADS // SSD_GROUPS

    rs = slice(c * CHUNK, (c + 1) * CHUNK)
    xs = xs_ref[rs, :]
    dt = dt_ref[rs, :]
    adt = dt * a_neg
    xdt = xs * dt
    a_cs = sum(_dot(tri, piece) for piece in _split3(adt))
    a_last = a_cs[CHUNK - 1:CHUNK, :]
    a_row = jnp.sum(jnp.where(diag_w, a_cs, 0.0), axis=0, keepdims=True)
    decay = jnp.exp2(jnp.where(causal_w, a_cs - a_row, -jnp.inf))
    decay_in = jnp.exp2(a_cs)
    decay_out = jnp.exp2(a_last - a_cs)
    chunk_decay = jnp.exp2(a_last)

    ys = []
    for g in range(SSD_GROUPS):
        gl = slice(g * GROUP_WIDTH, (g + 1) * GROUP_WIDTH)
        b_gb = bc_ref[rs, g * SSD_STATE:(g + 1) * SSD_STATE]
        c_gb = bc_ref[rs, (SSD_GROUPS + g) * SSD_STATE:(SSD_GROUPS + g + 1) * SSD_STATE]
        cb_rep = lax.dot_general(c_gb, jnp.concatenate([b_gb] * heads_per_group, axis=0), NT_DIMS,
                                 preferred_element_type=F32)
        m = (cb_rep * decay[:, gl]).astype(BF16)
        x_gb = xdt[:, gl].astype(BF16)
        x_blockdiag = jnp.concatenate(
            [jnp.where(lane_g >> HEAD_SHIFT == j, x_gb, jnp.zeros_like(x_gb)) for j in range(heads_per_group)],
            axis=0)
        y_diag = _dot(m, x_blockdiag)
        prev = state_scr[:, gl]
        y_off = _dot(c_gb, prev.astype(BF16)) * decay_in[:, gl]
        xd = (xdt[:, gl] * decay_out[:, gl]).astype(BF16)
        new_states = lax.dot_general(b_gb, xd, TN_DIMS, preferred_element_type=F32)
        state_scr[:, gl] = prev * chunk_decay[:, gl] + new_states
        ys.append(y_diag + y_off)
    y = (jnp.concatenate(ys, axis=1) + dskip * xs) * zs_ref[rs, :]
    ys_scr[rs, :] = (_rms_scale(y) * nw).astype(BF16)


def _ssd_kernel(xs_ref, bc_ref, dt_ref, zs_ref, alog_ref, dskip_ref, nw_ref, y_ref, state_scr):
    @pl.when(pl.program_id(1) == 0)
    def _():
        state_scr[...] = jnp.zeros_like(state_scr)

    a_neg = -jnp.exp(alog_ref[...]) * LOG2E
    dskip = dskip_ref[...]
    nw = nw_ref[...]
    for c in range(xs_ref.shape[0] // CHUNK):
        _ssd_chunk(c, xs_ref, bc_ref, dt_ref, zs_ref, a_neg, dskip, nw, y_ref, state_scr)


def _ssd_call(l, xs, bc, dt, zs, alog, dskip, nw, batch, seq):
    t = xs.shape[0]
    ts = SEQ_TILE
    nt = seq // ts
    tile = lambda w: pl.BlockSpec((ts, w), lambda b, i: (b * nt + i, 0))
    return pl.pallas_call(
        _ssd_kernel,
        grid=(batch, nt),
        in_specs=[tile(SSD_WIDTH), tile(XBC_DIM - SSD_WIDTH), tile(SSD_WIDTH), tile(SSD_WIDTH),
                  _layer(alog, l), _layer(dskip, l), _layer(nw, l)],
        out_specs=tile(SSD_WIDTH),
        out_shape=jax.ShapeDtypeStruct((t, SSD_WIDTH), BF16),
        scratch_shapes=[pltpu.VMEM((SSD_STATE, SSD_WIDTH), F32)],
        compiler_params=_params("arbitrary", "arbitrary"),
        name="ssd_scan",
    )(xs, bc, dt, zs, alog, dskip, nw)


def _attend(g, p, q_ref, ktp_ref, ktc_ref, vp_ref, vc_ref, bias_ref, ya_scr, first_tile):
    slab = ATTN_SLAB_HEADS * ATTN_HEAD_DIM
    lane_head = lax.broadcasted_iota(jnp.int32, (ATTN_GROUP_Q, slab), 1) >> HEAD_SHIFT
    lo = g * ATTN_GROUP_Q
    hi = lo + ATTN_GROUP_Q
    sl = slice(p * slab, (p + 1) * slab)
    cur = slice(max(lo - ATTN_HIST, 0), lo + ATTN_GROUP_K - ATTN_HIST)
    if lo < ATTN_HIST:
        kt = jnp.concatenate([ktp_ref[sl, lo:ATTN_HIST], ktc_ref[sl, cur]], axis=1)
        vs = jnp.concatenate([vp_ref[lo:ATTN_HIST, sl], vc_ref[cur, sl]], axis=0)
    else:
        kt, vs = ktc_ref[sl, cur], vc_ref[cur, sl]
    qg = q_ref[lo:hi, sl]
    zero = jnp.zeros_like(qg)
    qs = jnp.concatenate([jnp.where(lane_head == a, qg, zero) for a in range(ATTN_SLAB_HEADS)], axis=0)
    s = _dot(qs, kt) + bias_ref[p]
    if first_tile and lo < ATTN_HIST:
        band_pos = lax.broadcasted_iota(jnp.int32, (1, ATTN_GROUP_K), 1)
        s = jnp.where(band_pos >= ATTN_HIST - lo, s, jnp.finfo(F32).min)
    e = jnp.exp2(s - jnp.max(s, axis=-1, keepdims=True))
    denom = jnp.sum(e, axis=-1, keepdims=True)
    o = _dot(e.astype(BF16), vs) / denom
    og = o[:ATTN_GROUP_Q]
    for a in range(1, ATTN_SLAB_HEADS):
        og = jnp.where(lane_head == a, o[a * ATTN_GROUP_Q:(a + 1) * ATTN_GROUP_Q], og)
    ya_scr[lo:hi, sl] = og.astype(BF16)


def _attn_kernel(q_ref, ktp_ref, ktc_ref, vp_ref, vc_ref, bias_ref, ys_ref, x_ref, mod_ref, ws_ref, wa_ref,
                 o_ref, ya_scr):
    ts = q_ref.shape[0]
    slabs = ATTN_WIDTH // (ATTN_SLAB_HEADS * ATTN_HEAD_DIM)

    def attention(first_tile):
        for g in range(ts // ATTN_GROUP_Q):
            for p in range(slabs):
                _attend(g, p, q_ref, ktp_ref, ktc_ref, vp_ref, vc_ref, bias_ref, ya_scr, first_tile)

    first = pl.program_id(1) == 0
    pl.when(first)(lambda: attention(True))
    pl.when(jnp.logical_not(first))(lambda: attention(False))

    mix = _dot(ys_ref[...], ws_ref[...]) + _dot(ya_scr[...], wa_ref[...])
    o_ref[...] = x_ref[...] + mod_ref[0, 2:3, :] * mix


def _attn_call(l, q, kt, v, bias, ys, x, mod, wo, batch, seq):
    t, d = x.shape
    ts = SEQ_TILE
    nt = seq // ts
    per_tile = ts // ATTN_HIST
    hist = lambda b, i: jnp.maximum((b * nt + i) * per_tile - 1, 0)
    tile = lambda w: pl.BlockSpec((ts, w), lambda b, i: (b * nt + i, 0))
    prev = pl.BlockSpec((ATTN_HIST, ATTN_WIDTH), lambda b, i: (hist(b, i), 0))
    cur_t = pl.BlockSpec((ATTN_WIDTH, ts), lambda b, i: (0, b * nt + i))
    prev_t = pl.BlockSpec((ATTN_WIDTH, ATTN_HIST), lambda b, i: (0, hist(b, i)))
    return pl.pallas_call(
        _attn_kernel,
        grid=(batch, nt),
        in_specs=[tile(ATTN_WIDTH), prev_t, cur_t, prev, tile(ATTN_WIDTH), _layer(bias, l),
                  tile(SSD_WIDTH), tile(d), pl.BlockSpec((None, 1, 6, d), lambda b, i: (l, b, 0, 0)),
                  _layer(wo, l, (SSD_WIDTH, d), (0, 0)), _layer(wo, l, (ATTN_WIDTH, d), (1, 0))],
        out_specs=tile(d),
        out_shape=jax.ShapeDtypeStruct((t, d), F32),
        scratch_shapes=[pltpu.VMEM((ts, ATTN_WIDTH), BF16)],
        compiler_params=_params("arbitrary", "arbitrary"),
        name="attn_out",
    )(q, kt, kt, v, v, bias, ys, x, mod, wo, wo)


def _ffn_kernel(x_ref, mod_ref, nw_ref, wu_ref, cw_ref, wd_ref, fnw_ref, o_ref,
                h_scr, u_scr, act_scr, *, tiles_per_seq, final):
    tm = x_ref.shape[0]
    i = pl.program_id(0)
    x = x_ref[...]
    h = _rms_scale(x) * nw_ref[...]
    h = h * (1.0 + mod_ref[0, 4:5, :]) + mod_ref[0, 3:4, :]

    @pl.when(i % tiles_per_seq == 0)
    def _():
        h_scr[0:HALO, :] = jnp.zeros((HALO, D_MODEL), BF16)

    h_scr[HALO:HALO + tm, :] = h.astype(BF16)

    for c in range(FFN_DIM // FFN_CHUNK):
        gate = slice(c * FFN_CHUNK, (c + 1) * FFN_CHUNK)
        value = slice(FFN_DIM + c * FFN_CHUNK, FFN_DIM + (c + 1) * FFN_CHUNK)
        pick = lambda ref, rows: jnp.concatenate([ref[rows, gate], ref[rows, value]], axis=1)
        buf = u_scr.at[c % 2]
        buf[...] = _dot(h_scr[...], pick(wu_ref, slice(None)))
        u = buf[HALO:HALO + tm, :] * pick(cw_ref, slice(FFN_CONV - 1, FFN_CONV))
        for k in range(FFN_CONV - 1):
            off = HALO - (FFN_CONV - 1) + k
            u = u + buf[off:off + tm, :] * pick(cw_ref, slice(k, k + 1))
        act_scr[:, gate] = (_silu(u[:, :FFN_CHUNK]) * u[:, FFN_CHUNK:]).astype(BF16)

    h_scr[0:HALO, :] = h_scr[tm:tm + HALO, :]
    out = x + mod_ref[0, 5:6, :] * _dot(act_scr[...], wd_ref[...])
    if final:
        out = _rms_scale(out) * fnw_ref[...]
    o_ref[...] = out


def _ffn_call(l, x, mod, nw, wu, cw, wd, fnw, seq, final):
    t, d = x.shape
    tm = ROW_TILE
    tiles_per_seq = seq // tm
    row = pl.BlockSpec((tm, d), lambda i: (i, 0))
    return pl.pallas_call(
        functools.partial(_ffn_kernel, tiles_per_seq=tiles_per_seq, final=final),
        grid=(t // tm,),
        in_specs=[row, pl.BlockSpec((None, 1, 6, d), lambda i: (l, i // tiles_per_seq, 0, 0)), _layer(nw, l),
                  _layer(wu, l), _layer(cw, l), _layer(wd, l), _resident((1, d))],
        out_specs=row,
        out_shape=jax.ShapeDtypeStruct((t, d), F32),
        scratch_shapes=[
            pltpu.VMEM((tm + HALO, d), BF16),
            pltpu.VMEM((2, tm + HALO, 2 * FFN_CHUNK), F32),
            pltpu.VMEM((tm, FFN_DIM), BF16),
        ],
        compiler_params=_params("arbitrary"),
        name="conv_ffn",
    )(x, mod, nw, wu, cw, wd, fnw)


def _rel_bias_table(rel_bias):
    depth, heads = rel_bias.shape[:2]
    rb = rel_bias.astype(F32).reshape(depth * heads, -1) * LOG2E
    period = ATTN_GROUP_Q + ATTN_GROUP_K
    far = rb[:, 2 * MAX_REL:]
    flat_len = LEFT_CHUNKS * CHUNK - MAX_REL
    unseen = period - flat_len - 2 * MAX_REL
    w = jnp.concatenate([jnp.repeat(far, flat_len, axis=1), rb[:, :0:-1], jnp.repeat(far, unseen, axis=1)], axis=1)
    assert w.shape[1] == period
    toep = jnp.tile(w, (1, ATTN_GROUP_Q))[:, :ATTN_GROUP_Q * (period - 1)]
    toep = toep.reshape(depth * heads, ATTN_GROUP_Q, period - 1)[:, :, :ATTN_GROUP_K]
    first = (np.arange(ATTN_GROUP_Q)[:, None] // CHUNK) * CHUNK
    pos = np.arange(ATTN_GROUP_K)[None, :]
    window = (pos >= first) & (pos < first + BAND)
    table = jnp.where(window[None], toep, -1e30)
    return table.reshape(depth, heads // ATTN_SLAB_HEADS, ATTN_SLAB_HEADS * ATTN_GROUP_Q, ATTN_GROUP_K)


def _per_lane(v):
    return jnp.repeat(v.astype(F32), SSD_HEAD_DIM, axis=1)[:, None, :]


def kernel(x, c, norm_mix_w, w_ada, b_ada, w_in, ssd_conv_w, ssd_conv_b, dt_bias, a_log, d_skip, ssd_norm_w,
           rel_bias, w_out, norm_ffn_w, w_up, ffn_conv_w, w_down, final_norm_w):
    batch, seq, d = x.shape
    depth = w_ada.shape[0]
    assert d == D_MODEL and seq % SEQ_TILE == 0 and seq % ROW_TILE == 0
    mod = _ada_call(c, w_ada, b_ada).reshape(depth, batch, 6, d)

    o_xbc, o_dt, o_q = SSD_WIDTH, SSD_WIDTH + XBC_DIM, SSD_WIDTH + XBC_DIM + SSD_HEADS
    o_k, o_v = o_q + ATTN_WIDTH, o_q + 2 * ATTN_WIDTH
    cols = lambda lo, hi: w_in[:, :, lo:hi].astype(BF16)
    wz, wxbc, wq, wv = cols(0, o_xbc), cols(o_xbc, o_dt), cols(o_q, o_k), cols(o_v, None)
    wdt = jnp.repeat(cols(o_dt, o_q), SSD_HEAD_DIM, axis=2)
    wkt = jnp.swapaxes(cols(o_k, o_v), 1, 2)
    wo, wu, wd = w_out.astype(BF16), w_up.astype(BF16), w_down.astype(BF16)
    row3 = lambda a: a[:, None, :]
    nmw, nfw, cb, snw = row3(norm_mix_w), row3(norm_ffn_w), row3(ssd_conv_b), row3(ssd_norm_w)
    dtb, alog, dskip = _per_lane(dt_bias), _per_lane(a_log), _per_lane(d_skip)
    bias = _rel_bias_table(rel_bias)

    xf = x.reshape(batch * seq, d)
    for l in range(depth):
        zs, xs, bc, dt, q, kt, v = _inproj_call(l, xf, mod, nmw, wz, wxbc, wdt, wq, wkt, wv, ssd_conv_w, cb, dtb, seq)
        y_ssd = _ssd_call(l, xs, bc, dt, zs, alog, dskip, snw, batch, seq)
        xf = _attn_call(l, q, kt, v, bias, y_ssd, xf, mod, wo, batch, seq)
        xf = _ffn_call(l, xf, mod, nfw, wu, ffn_conv_w, wd, final_norm_w[None, :], seq, final=(l == depth - 1))
    return xf.reshape(batch, seq, d)
```

```python
import functools
import math

import numpy as np
import jax
import jax.numpy as jnp
from jax import lax
from jax.experimental import pallas as pl
from jax.experimental.pallas import tpu as pltpu

F32 = jnp.float32
BF16 = jnp.bfloat16

D_MODEL = 1024
CHUNK = 64
SSD_WIDTH = 512
SSD_HEAD_DIM = 64
SSD_HEADS = 8
SSD_GROUPS = 2
SSD_STATE = 128
SSD_CONV = 4
XBC_DIM = SSD_WIDTH + 2 * SSD_GROUPS * SSD_STATE
GROUP_WIDTH = SSD_WIDTH // SSD_GROUPS
ATTN_WIDTH = 512
ATTN_HEAD_DIM = 64
LEFT_CHUNKS = 8
BAND = (LEFT_CHUNKS + 1) * CHUNK
MAX_REL = 128
FFN_DIM = 2816
FFN_CONV = 3
EPS = 1e-6
LOG2E = math.log2(math.e)

HEAD_SHIFT = 6
SUBLANES = 8
ROW_TILE = 1024
SEQ_TILE = 1024
ATTN_GROUP_Q = 2 * CHUNK
ATTN_HIST = LEFT_CHUNKS * CHUNK
ATTN_GROUP_K = ATTN_GROUP_Q + ATTN_HIST
ATTN_SLAB_HEADS = 4
FFN_CHUNK = 256
HALO = 16
VMEM_LIMIT = 56 * 1024 * 1024

NT_DIMS = (((1,), (1,)), ((), ()))
TN_DIMS = (((0,), (0,)), ((), ()))


def _dot(a, b):
    return jnp.dot(a, b, preferred_element_type=F32)


def _silu(x):
    half = 0.5 * x
    return half + half * jnp.tanh(half)


def _softplus(x):
    return jnp.maximum(x, 0.0) + jnp.log(1.0 + jnp.exp(-jnp.abs(x)))


def _rms_scale(x):
    return x * lax.rsqrt(jnp.mean(x * x, axis=-1, keepdims=True) + EPS)


def _params(*semantics):
    return pltpu.CompilerParams(dimension_semantics=semantics, vmem_limit_bytes=VMEM_LIMIT)


def _resident(shape):
    return pl.BlockSpec(shape, lambda *_: (0,) * len(shape), pipeline_mode=pl.Buffered(1))


def _layer(arr, l, block=None, index=None):
    block = tuple(arr.shape[1:]) if block is None else block
    index = (0,) * len(block) if index is None else index
    return pl.BlockSpec((None,) + block, lambda *_: (l,) + index, pipeline_mode=pl.Buffered(1))


def _ada_kernel(c_ref, w_ref, b_ref, o_ref):
    c = c_ref[...]
    c_act = _silu(c).astype(BF16)
    o_ref[0] = _dot(c_act, w_ref[0].astype(BF16)) + b_ref[0]


def _ada_call(c, w_ada, b_ada):
    depth, d, n = w_ada.shape
    b = c.shape[0]
    tn = n // 4
    return pl.pallas_call(
        _ada_kernel,
        grid=(depth, n // tn),
        in_specs=[
            pl.BlockSpec((b, d), lambda l, j: (0, 0)),
            pl.BlockSpec((1, d, tn), lambda l, j: (l, 0, j)),
            pl.BlockSpec((1, 1, tn), lambda l, j: (l, 0, j)),
        ],
        out_specs=pl.BlockSpec((1, b, tn), lambda l, j: (l, 0, j)),
        out_shape=jax.ShapeDtypeStruct((depth, b, n), F32),
        compiler_params=_params("arbitrary", "arbitrary"),
        name="ada_mod",
    )(c, w_ada, b_ada.reshape(depth, 1, n))


def _inproj_kernel(x_ref, mod_ref, nw_ref, wz_ref, wxbc_ref, wdt_ref, wq_ref, wkt_ref, wv_ref,
                   cw_ref, cb_ref, dtb_ref,
                   zs_ref, xs_ref, bc_ref, dt_ref, q_ref, kt_ref, v_ref, conv_scr, *, tiles_per_seq):
    tm = x_ref.shape[0]
    h = _rms_scale(x_ref[...]) * nw_ref[...]
    h = h * (1.0 + mod_ref[0, 1:2, :]) + mod_ref[0, 0:1, :]
    hb = h.astype(BF16)

    zs_ref[...] = _silu(_dot(hb, wz_ref[...]))

    @pl.when(pl.program_id(0) % tiles_per_seq == 0)
    def _():
        conv_scr[...] = jnp.zeros((SUBLANES, XBC_DIM), F32)

    xbc = _dot(hb, wxbc_ref[...])
    ext = jnp.concatenate([conv_scr[...], xbc], axis=0)
    acc = cb_ref[...] + xbc * cw_ref[SSD_CONV - 1:SSD_CONV, :]
    for k in range(SSD_CONV - 1):
        off = SUBLANES - (SSD_CONV - 1) + k
        acc = acc + ext[off:off + tm, :] * cw_ref[k:k + 1, :]
    xc = _silu(acc)
    xs_ref[...] = xc[:, :SSD_WIDTH]
    bc_ref[...] = xc[:, SSD_WIDTH:].astype(BF16)
    conv_scr[...] = xbc[tm - SUBLANES:tm, :]

    dt_ref[...] = _softplus(_dot(hb, wdt_ref[...]) + dtb_ref[...])
    q_ref[...] = (_dot(hb, wq_ref[...]) * (LOG2E * ATTN_HEAD_DIM ** -0.5)).astype(BF16)
    kt_ref[...] = lax.dot_general(wkt_ref[...], hb, NT_DIMS, preferred_element_type=F32).astype(BF16)
    v_ref[...] = _dot(hb, wv_ref[...]).astype(BF16)


def _inproj_call(l, x, mod, nw, wz, wxbc, wdt, wq, wkt, wv, cw, cb, dtb, seq):
    t, d = x.shape
    tm = ROW_TILE
    tiles_per_seq = seq // tm
    row = lambda w: pl.BlockSpec((tm, w), lambda i: (i, 0))
    col = pl.BlockSpec((ATTN_WIDTH, tm), lambda i: (0, i))
    sds = jax.ShapeDtypeStruct
    return pl.pallas_call(
        functools.partial(_inproj_kernel, tiles_per_seq=tiles_per_seq),
        grid=(t // tm,),
        in_specs=[
            row(d),
            pl.BlockSpec((None, 1, 6, d), lambda i: (l, i // tiles_per_seq, 0, 0)),
            _layer(nw, l),
            _layer(wz, l), _layer(wxbc, l), _layer(wdt, l), _layer(wq, l), _layer(wkt, l), _layer(wv, l),
            _layer(cw, l), _layer(cb, l), _layer(dtb, l),
        ],
        out_specs=[row(SSD_WIDTH), row(SSD_WIDTH), row(XBC_DIM - SSD_WIDTH), row(SSD_WIDTH),
                   row(ATTN_WIDTH), col, row(ATTN_WIDTH)],
        out_shape=[sds((t, SSD_WIDTH), F32), sds((t, SSD_WIDTH), F32), sds((t, XBC_DIM - SSD_WIDTH), BF16),
                   sds((t, SSD_WIDTH), F32),
                   sds((t, ATTN_WIDTH), BF16), sds((ATTN_WIDTH, t), BF16), sds((t, ATTN_WIDTH), BF16)],
        scratch_shapes=[pltpu.VMEM((SUBLANES, XBC_DIM), F32)],
        compiler_params=_params("arbitrary"),
        name="in_proj",
    )(x, mod, nw, wz, wxbc, wdt, wq, wkt, wv, cw, cb, dtb)


def _split3(a):
    hi = a.astype(BF16)
    r1 = a - hi.astype(F32)
    mid = r1.astype(BF16)
    lo = (r1 - mid.astype(F32)).astype(BF16)
    return hi, mid, lo


def _ssd_chunk(c, xs_ref, bc_ref, dt_ref, zs_ref, a_neg, dskip, nw, ys_scr, state_scr):
    rows = lax.broadcasted_iota(jnp.int32, (CHUNK, CHUNK), 0)
    cols = lax.broadcasted_iota(jnp.int32, (CHUNK, CHUNK), 1)
    tri = (rows >= cols).astype(BF16)
    row_w = lax.broadcasted_iota(jnp.int32, (CHUNK, SSD_WIDTH), 0)
    lane_w = lax.broadcasted_iota(jnp.int32, (CHUNK, SSD_WIDTH), 1)
    pos_w = lane_w & (SSD_HEAD_DIM - 1)
    diag_w = row_w == pos_w
    causal_w = row_w >= pos_w
    lane_g = lax.broadcasted_iota(jnp.int32, (CHUNK, GROUP_WIDTH), 1)
    heads_per_group = SSD_HEADS // SSD_GROUPS

    rs = slice(c * CHUNK, (c + 1) * CHUNK)
    xs = xs_ref[rs, :]
    dt = dt_ref[rs, :]
    adt = dt * a_neg
    xdt = xs * dt
    a_cs = sum(_dot(tri, piece) for piece in _split3(adt))
    a_last = a_cs[CHUNK - 1:CHUNK, :]
    a_row = jnp.sum(jnp.where(diag_w, a_cs, 0.0), axis=0, keepdims=True)
    decay = jnp.exp2(jnp.where(causal_w, a_cs - a_row, -jnp.inf))
    decay_in = jnp.exp2(a_cs)
    decay_out = jnp.exp2(a_last - a_cs)
    chunk_decay = jnp.exp2(a_last)

    ys = []
    for g in range(SSD_GROUPS):
        gl = slice(g * GROUP_WIDTH, (g + 1) * GROUP_WIDTH)
        b_gb = bc_ref[rs, g * SSD_STATE:(g + 1) * SSD_STATE]
        c_gb = bc_ref[rs, (SSD_GROUPS + g) * SSD_STATE:(SSD_GROUPS + g + 1) * SSD_STATE]
        cb_rep = lax.dot_general(c_gb, jnp.concatenate([b_gb] * heads_per_group, axis=0), NT_DIMS,
                                 preferred_element_type=F32)
        m = (cb_rep * decay[:, gl]).astype(BF16)
        x_gb = xdt[:, gl].astype(BF16)
        x_blockdiag = jnp.concatenate(
            [jnp.where(lane_g >> HEAD_SHIFT == j, x_gb, jnp.zeros_like(x_gb)) for j in range(heads_per_group)],
            axis=0)
        y_diag = _dot(m, x_blockdiag)
        prev = state_scr[:, gl]
        y_off = _dot(c_gb, prev.astype(BF16)) * decay_in[:, gl]
        xd = (xdt[:, gl] * decay_out[:, gl]).astype(BF16)
        new_states = lax.dot_general(b_gb, xd, TN_DIMS, preferred_element_type=F32)
        state_scr[:, gl] = prev * chunk_decay[:, gl] + new_states
        ys.append(y_diag + y_off)
    y = (jnp.concatenate(ys, axis=1) + dskip * xs) * zs_ref[rs, :]
    ys_scr[rs, :] = (_rms_scale(y) * nw).astype(BF16)


def _ssd_kernel(xs_ref, bc_ref, dt_ref, zs_ref, alog_ref, dskip_ref, nw_ref, y_ref, state_scr):
    @pl.when(pl.program_id(1) == 0)
    def _():
        state_scr[...] = jnp.zeros_like(state_scr)

    a_neg = -jnp.exp(alog_ref[...]) * LOG2E
    dskip = dskip_ref[...]
    nw = nw_ref[...]
    for c in range(xs_ref.shape[0] // CHUNK):
        _ssd_chunk(c, xs_ref, bc_ref, dt_ref, zs_ref, a_neg, dskip, nw, y_ref, state_scr)


def _ssd_call(l, xs, bc, dt, zs, alog, dskip, nw, batch, seq):
    t = xs.shape[0]
    ts = SEQ_TILE
    nt = seq // ts
    tile = lambda w: pl.BlockSpec((ts, w), lambda b, i: (b * nt + i, 0))
    return pl.pallas_call(
        _ssd_kernel,
        grid=(batch, nt),
        in_specs=[tile(SSD_WIDTH), tile(XBC_DIM - SSD_WIDTH), tile(SSD_WIDTH), tile(SSD_WIDTH),
                  _layer(alog, l), _layer(dskip, l), _layer(nw, l)],
        out_specs=tile(SSD_WIDTH),
        out_shape=jax.ShapeDtypeStruct((t, SSD_WIDTH), BF16),
        scratch_shapes=[pltpu.VMEM((SSD_STATE, SSD_WIDTH), F32)],
        compiler_params=_params("arbitrary", "arbitrary"),
        name="ssd_scan",
    )(xs, bc, dt, zs, alog, dskip, nw)


def _attend(g, p, q_ref, ktp_ref, ktc_ref, vp_ref, vc_ref, bias_ref, ya_scr, first_tile):
    slab = ATTN_SLAB_HEADS * ATTN_HEAD_DIM
    lane_head = lax.broadcasted_iota(jnp.int32, (ATTN_GROUP_Q, slab), 1) >> HEAD_SHIFT
    lo = g * ATTN_GROUP_Q
    hi = lo + ATTN_GROUP_Q
    sl = slice(p * slab, (p + 1) * slab)
    cur = slice(max(lo - ATTN_HIST, 0), lo + ATTN_GROUP_K - ATTN_HIST)
    if lo < ATTN_HIST:
        kt = jnp.concatenate([ktp_ref[sl, lo:ATTN_HIST], ktc_ref[sl, cur]], axis=1)
        vs = jnp.concatenate([vp_ref[lo:ATTN_HIST, sl], vc_ref[cur, sl]], axis=0)
    else:
        kt, vs = ktc_ref[sl, cur], vc_ref[cur, sl]
    qg = q_ref[lo:hi, sl]
    zero = jnp.zeros_like(qg)
    qs = jnp.concatenate([jnp.where(lane_head == a, qg, zero) for a in range(ATTN_SLAB_HEADS)], axis=0)
    s = _dot(qs, kt) + bias_ref[p]
    if first_tile and lo < ATTN_HIST:
        band_pos = lax.broadcasted_iota(jnp.int32, (1, ATTN_GROUP_K), 1)
        s = jnp.where(band_pos >= ATTN_HIST - lo, s, jnp.finfo(F32).min)
    e = jnp.exp2(s - jnp.max(s, axis=-1, keepdims=True))
    denom = jnp.sum(e, axis=-1, keepdims=True)
    o = _dot(e.astype(BF16), vs) / denom
    og = o[:ATTN_GROUP_Q]
    for a in range(1, ATTN_SLAB_HEADS):
        og = jnp.where(lane_head == a, o[a * ATTN_GROUP_Q:(a + 1) * ATTN_GROUP_Q], og)
    ya_scr[lo:hi, sl] = og.astype(BF16)


def _attn_kernel(q_ref, ktp_ref, ktc_ref, vp_ref, vc_ref, bias_ref, ys_ref, x_ref, mod_ref, ws_ref, wa_ref,
                 o_ref, ya_scr):
    ts = q_ref.shape[0]
    slabs = ATTN_WIDTH // (ATTN_SLAB_HEADS * ATTN_HEAD_DIM)

    def attention(first_tile):
        for g in range(ts // ATTN_GROUP_Q):
            for p in range(slabs):
                _attend(g, p, q_ref, ktp_ref, ktc_ref, vp_ref, vc_ref, bias_ref, ya_scr, first_tile)

    first = pl.program_id(1) == 0
    pl.when(first)(lambda: attention(True))
    pl.when(jnp.logical_not(first))(lambda: attention(False))

    mix = _dot(ys_ref[...], ws_ref[...]) + _dot(ya_scr[...], wa_ref[...])
    o_ref[...] = x_ref[...] + mod_ref[0, 2:3, :] * mix


def _attn_call(l, q, kt, v, bias, ys, x, mod, wo, batch, seq):
    t, d = x.shape
    ts = SEQ_TILE
    nt = seq // ts
    per_tile = ts // ATTN_HIST
    hist = lambda b, i: jnp.maximum((b * nt + i) * per_tile - 1, 0)
    tile = lambda w: pl.BlockSpec((ts, w), lambda b, i: (b * nt + i, 0))
    prev = pl.BlockSpec((ATTN_HIST, ATTN_WIDTH), lambda b, i: (hist(b, i), 0))
    cur_t = pl.BlockSpec((ATTN_WIDTH, ts), lambda b, i: (0, b * nt + i))
    prev_t = pl.BlockSpec((ATTN_WIDTH, ATTN_HIST), lambda b, i: (0, hist(b, i)))
    return pl.pallas_call(
        _attn_kernel,
        grid=(batch, nt),
        in_specs=[tile(ATTN_WIDTH), prev_t, cur_t, prev, tile(ATTN_WIDTH), _layer(bias, l),
                  tile(SSD_WIDTH), tile(d), pl.BlockSpec((None, 1, 6, d), lambda b, i: (l, b, 0, 0)),
                  _layer(wo, l, (SSD_WIDTH, d), (0, 0)), _layer(wo, l, (ATTN_WIDTH, d), (1, 0))],
        out_specs=tile(d),
        out_shape=jax.ShapeDtypeStruct((t, d), F32),
        scratch_shapes=[pltpu.VMEM((ts, ATTN_WIDTH), BF16)],
        compiler_params=_params("arbitrary", "arbitrary"),
        name="attn_out",
    )(q, kt, kt, v, v, bias, ys, x, mod, wo, wo)


def _ffn_kernel(x_ref, mod_ref, nw_ref, wu_ref, cw_ref, wd_ref, fnw_ref, o_ref,
                h_scr, u_scr, act_scr, *, tiles_per_seq, final):
    tm = x_ref.shape[0]
    i = pl.program_id(0)
    x = x_ref[...]
    h = _rms_scale(x) * nw_ref[...]
    h = h * (1.0 + mod_ref[0, 4:5, :]) + mod_ref[0, 3:4, :]

    @pl.when(i % tiles_per_seq == 0)
    def _():
        h_scr[0:HALO, :] = jnp.zeros((HALO, D_MODEL), BF16)

    h_scr[HALO:HALO + tm, :] = h.astype(BF16)

    for c in range(FFN_DIM // FFN_CHUNK):
        gate = slice(c * FFN_CHUNK, (c + 1) * FFN_CHUNK)
        value = slice(FFN_DIM + c * FFN_CHUNK, FFN_DIM + (c + 1) * FFN_CHUNK)
        pick = lambda ref, rows: jnp.concatenate([ref[rows, gate], ref[rows, value]], axis=1)
        buf = u_scr.at[c % 2]
        buf[...] = _dot(h_scr[...], pick(wu_ref, slice(None)))
        u = buf[HALO:HALO + tm, :] * pick(cw_ref, slice(FFN_CONV - 1, FFN_CONV))
        for k in range(FFN_CONV - 1):
            off = HALO - (FFN_CONV - 1) + k
            u = u + buf[off:off + tm, :] * pick(cw_ref, slice(k, k + 1))
        act_scr[:, gate] = (_silu(u[:, :FFN_CHUNK]) * u[:, FFN_CHUNK:]).astype(BF16)

    h_scr[0:HALO, :] = h_scr[tm:tm + HALO, :]
    out = x + mod_ref[0, 5:6, :] * _dot(act_scr[...], wd_ref[...])
    if final:
        out = _rms_scale(out) * fnw_ref[...]
    o_ref[...] = out


def _ffn_call(l, x, mod, nw, wu, cw, wd, fnw, seq, final):
    t, d = x.shape
    tm = ROW_TILE
    tiles_per_seq = seq // tm
    row = pl.BlockSpec((tm, d), lambda i: (i, 0))
    return pl.pallas_call(
        functools.partial(_ffn_kernel, tiles_per_seq=tiles_per_seq, final=final),
        grid=(t // tm,),
        in_specs=[row, pl.BlockSpec((None, 1, 6, d), lambda i: (l, i // tiles_per_seq, 0, 0)), _layer(nw, l),
                  _layer(wu, l), _layer(cw, l), _layer(wd, l), _resident((1, d))],
        out_specs=row,
        out_shape=jax.ShapeDtypeStruct((t, d), F32),
        scratch_shapes=[
            pltpu.VMEM((tm + HALO, d), BF16),
            pltpu.VMEM((2, tm + HALO, 2 * FFN_CHUNK), F32),
            pltpu.VMEM((tm, FFN_DIM), BF16),
        ],
        compiler_params=_params("arbitrary"),
        name="conv_ffn",
    )(x, mod, nw, wu, cw, wd, fnw)


def _rel_bias_table(rel_bias):
    depth, heads = rel_bias.shape[:2]
    rb = rel_bias.astype(F32).reshape(depth * heads, -1) * LOG2E
    period = ATTN_GROUP_Q + ATTN_GROUP_K
    far = rb[:, 2 * MAX_REL:]
    flat_len = LEFT_CHUNKS * CHUNK - MAX_REL
    unseen = period - flat_len - 2 * MAX_REL
    w = jnp.concatenate([jnp.repeat(far, flat_len, axis=1), rb[:, :0:-1], jnp.repeat(far, unseen, axis=1)], axis=1)
    assert w.shape[1] == period
    toep = jnp.tile(w, (1, ATTN_GROUP_Q))[:, :ATTN_GROUP_Q * (period - 1)]
    toep = toep.reshape(depth * heads, ATTN_GROUP_Q, period - 1)[:, :, :ATTN_GROUP_K]
    first = (np.arange(ATTN_GROUP_Q)[:, None] // CHUNK) * CHUNK
    pos = np.arange(ATTN_GROUP_K)[None, :]
    window = (pos >= first) & (pos < first + BAND)
    table = jnp.where(window[None], toep, -1e30)
    return table.reshape(depth, heads // ATTN_SLAB_HEADS, ATTN_SLAB_HEADS * ATTN_GROUP_Q, ATTN_GROUP_K)


def _per_lane(v):
    return jnp.repeat(v.astype(F32), SSD_HEAD_DIM, axis=1)[:, None, :]


def kernel(x, c, norm_mix_w, w_ada, b_ada, w_in, ssd_conv_w, ssd_conv_b, dt_bias, a_log, d_skip, ssd_norm_w,
           rel_bias, w_out, norm_ffn_w, w_up, ffn_conv_w, w_down, final_norm_w):
    batch, seq, d = x.shape
    depth = w_ada.shape[0]
    assert d == D_MODEL and seq % SEQ_TILE == 0 and seq % ROW_TILE == 0
    mod = _ada_call(c, w_ada, b_ada).reshape(depth, batch, 6, d)

    o_xbc, o_dt, o_q = SSD_WIDTH, SSD_WIDTH + XBC_DIM, SSD_WIDTH + XBC_DIM + SSD_HEADS
    o_k, o_v = o_q + ATTN_WIDTH, o_q + 2 * ATTN_WIDTH
    cols = lambda lo, hi: w_in[:, :, lo:hi].astype(BF16)
    wz, wxbc, wq, wv = cols(0, o_xbc), cols(o_xbc, o_dt), cols(o_q, o_k), cols(o_v, None)
    wdt = jnp.repeat(cols(o_dt, o_q), SSD_HEAD_DIM, axis=2)
    wkt = jnp.swapaxes(cols(o_k, o_v), 1, 2)
    wo, wu, wd = w_out.astype(BF16), w_up.astype(BF16), w_down.astype(BF16)
    row3 = lambda a: a[:, None, :]
    nmw, nfw, cb, snw = row3(norm_mix_w), row3(norm_ffn_w), row3(ssd_conv_b), row3(ssd_norm_w)
    dtb, alog, dskip = _per_lane(dt_bias), _per_lane(a_log), _per_lane(d_skip)
    bias = _rel_bias_table(rel_bias)

    xf = x.reshape(batch * seq, d)
    for l in range(depth):
        zs, xs, bc, dt, q, kt, v = _inproj_call(l, xf, mod, nmw, wz, wxbc, wdt, wq, wkt, wv, ssd_conv_w, cb, dtb, seq)
        y_ssd = _ssd_call(l, xs, bc, dt, zs, alog, dskip, snw, batch, seq)
        xf = _attn_call(l, q, kt, v, bias, y_ssd, xf, mod, wo, batch, seq)
        xf = _ffn_call(l, xf, mod, nfw, wu, ffn_conv_w, wd, final_norm_w[None, :], seq, final=(l == depth - 1))
    return xf.reshape(batch, seq, d)
```

```python
import functools
import math

import numpy as np
import jax
import jax.numpy as jnp
from jax import lax
from jax.experimental import pallas as pl
from jax.experimental.pallas import tpu as pltpu

F32 = jnp.float32
BF16 = jnp.bfloat16

D_MODEL = 1024
CHUNK = 64
SSD_WIDTH = 512
SSD_HEAD_DIM = 64
SSD_HEADS = 8
SSD_GROUPS = 2
SSD_STATE = 128
SSD_CONV = 4
XBC_DIM = SSD_WIDTH + 2 * SSD_GROUPS * SSD_STATE
GROUP_WIDTH = SSD_WIDTH // SSD_GROUPS
ATTN_WIDTH = 512
ATTN_HEAD_DIM = 64
LEFT_CHUNKS = 8
BAND = (LEFT_CHUNKS + 1) * CHUNK
MAX_REL = 128
FFN_DIM = 2816
FFN_CONV = 3
EPS = 1e-6
LOG2E = math.log2(math.e)

HEAD_SHIFT = 6
SUBLANES = 8
ROW_TILE = 1024
SEQ_TILE = 1024
ATTN_GROUP_Q = 2 * CHUNK
ATTN_HIST = LEFT_CHUNKS * CHUNK
ATTN_GROUP_K = ATTN_GROUP_Q + ATTN_HIST
ATTN_SLAB_HEADS = 4
FFN_CHUNK = 256
HALO = 16
VMEM_LIMIT = 56 * 1024 * 1024

NT_DIMS = (((1,), (1,)), ((), ()))
TN_DIMS = (((0,), (0,)), ((), ()))


def _dot(a, b):
    return jnp.dot(a, b, preferred_element_type=F32)


def _silu(x):
    half = 0.5 * x
    return half + half * jnp.tanh(half)


def _softplus(x):
    return jnp.maximum(x, 0.0) + jnp.log(1.0 + jnp.exp(-jnp.abs(x)))


def _rms_scale(x):
    return x * lax.rsqrt(jnp.mean(x * x, axis=-1, keepdims=True) + EPS)


def _params(*semantics):
    return pltpu.CompilerParams(dimension_semantics=semantics, vmem_limit_bytes=VMEM_LIMIT)


def _resident(shape):
    return pl.BlockSpec(shape, lambda *_: (0,) * len(shape), pipeline_mode=pl.Buffered(1))


def _layer(arr, l, block=None, index=None):
    block = tuple(arr.shape[1:]) if block is None else block
    index = (0,) * len(block) if index is None else index
    return pl.BlockSpec((None,) + block, lambda *_: (l,) + index, pipeline_mode=pl.Buffered(1))


def _ada_kernel(c_ref, w_ref, b_ref, o_ref):
    c = c_ref[...]
    c_act = _silu(c).astype(BF16)
    o_ref[0] = _dot(c_act, w_ref[0].astype(BF16)) + b_ref[0]


def _ada_call(c, w_ada, b_ada):
    depth, d, n = w_ada.shape
    b = c.shape[0]
    tn = n // 4
    return pl.pallas_call(
        _ada_kernel,
        grid=(depth, n // tn),
        in_specs=[
            pl.BlockSpec((b, d), lambda l, j: (0, 0)),
            pl.BlockSpec((1, d, tn), lambda l, j: (l, 0, j)),
            pl.BlockSpec((1, 1, tn), lambda l, j: (l, 0, j)),
        ],
        out_specs=pl.BlockSpec((1, b, tn), lambda l, j: (l, 0, j)),
        out_shape=jax.ShapeDtypeStruct((depth, b, n), F32),
        compiler_params=_params("arbitrary", "arbitrary"),
        name="ada_mod",
    )(c, w_ada, b_ada.reshape(depth, 1, n))


def _inproj_kernel(x_ref, mod_ref, nw_ref, wz_ref, wxbc_ref, wdt_ref, wq_ref, wkt_ref, wv_ref,
                   cw_ref, cb_ref, dtb_ref,
                   zs_ref, xs_ref, bc_ref, dt_ref, q_ref, kt_ref, v_ref, conv_scr, *, tiles_per_seq):
    tm = x_ref.shape[0]
    h = _rms_scale(x_ref[...]) * nw_ref[...]
    h = h * (1.0 + mod_ref[0, 1:2, :]) + mod_ref[0, 0:1, :]
    hb = h.astype(BF16)

    zs_ref[...] = _silu(_dot(hb, wz_ref[...]))

    @pl.when(pl.program_id(0) % tiles_per_seq == 0)
    def _():
        conv_scr[...] = jnp.zeros((SUBLANES, XBC_DIM), F32)

    xbc = _dot(hb, wxbc_ref[...])
    ext = jnp.concatenate([conv_scr[...], xbc], axis=0)
    acc = cb_ref[...] + xbc * cw_ref[SSD_CONV - 1:SSD_CONV, :]
    for k in range(SSD_CONV - 1):
        off = SUBLANES - (SSD_CONV - 1) + k
        acc = acc + ext[off:off + tm, :] * cw_ref[k:k + 1, :]
    xc = _silu(acc)
    xs_ref[...] = xc[:, :SSD_WIDTH]
    bc_ref[...] = xc[:, SSD_WIDTH:].astype(BF16)
    conv_scr[...] = xbc[tm - SUBLANES:tm, :]

    dt_ref[...] = _softplus(_dot(hb, wdt_ref[...]) + dtb_ref[...])
    q_ref[...] = (_dot(hb, wq_ref[...]) * (LOG2E * ATTN_HEAD_DIM ** -0.5)).astype(BF16)
    kt_ref[...] = lax.dot_general(wkt_ref[...], hb, NT_DIMS, preferred_element_type=F32).astype(BF16)
    v_ref[...] = _dot(hb, wv_ref[...]).astype(BF16)


def _inproj_call(l, x, mod, nw, wz, wxbc, wdt, wq, wkt, wv, cw, cb, dtb, seq):
    t, d = x.shape
    tm = ROW_TILE
    tiles_per_seq = seq // tm
    row = lambda w: pl.BlockSpec((tm, w), lambda i: (i, 0))
    col = pl.BlockSpec((ATTN_WIDTH, tm), lambda i: (0, i))
    sds = jax.ShapeDtypeStruct
    return pl.pallas_call(
        functools.partial(_inproj_kernel, tiles_per_seq=tiles_per_seq),
        grid=(t // tm,),
        in_specs=[
            row(d),
            pl.BlockSpec((None, 1, 6, d), lambda i: (l, i // tiles_per_seq, 0, 0)),
            _layer(nw, l),
            _layer(wz, l), _layer(wxbc, l), _layer(wdt, l), _layer(wq, l), _layer(wkt, l), _layer(wv, l),
            _layer(cw, l), _layer(cb, l), _layer(dtb, l),
        ],
        out_specs=[row(SSD_WIDTH), row(SSD_WIDTH), row(XBC_DIM - SSD_WIDTH), row(SSD_WIDTH),
                   row(ATTN_WIDTH), col, row(ATTN_WIDTH)],
        out_shape=[sds((t, SSD_WIDTH), F32), sds((t, SSD_WIDTH), F32), sds((t, XBC_DIM - SSD_WIDTH), BF16),
                   sds((t, SSD_WIDTH), F32),
                   sds((t, ATTN_WIDTH), BF16), sds((ATTN_WIDTH, t), BF16), sds((t, ATTN_WIDTH), BF16)],
        scratch_shapes=[pltpu.VMEM((SUBLANES, XBC_DIM), F32)],
        compiler_params=_params("arbitrary"),
        name="in_proj",
    )(x, mod, nw, wz, wxbc, wdt, wq, wkt, wv, cw, cb, dtb)


def _split3(a):
    hi = a.astype(BF16)
    r1 = a - hi.astype(F32)
    mid = r1.astype(BF16)
    lo = (r1 - mid.astype(F32)).astype(BF16)
    return hi, mid, lo


def _ssd_chunk(c, xs_ref, bc_ref, dt_ref, zs_ref, a_neg, dskip, nw, ys_scr, state_scr):
    rows = lax.broadcasted_iota(jnp.int32, (CHUNK, CHUNK), 0)
    cols = lax.broadcasted_iota(jnp.int32, (CHUNK, CHUNK), 1)
    tri = (rows >= cols).astype(BF16)
    row_w = lax.broadcasted_iota(jnp.int32, (CHUNK, SSD_WIDTH), 0)
    lane_w = lax.broadcasted_iota(jnp.int32, (CHUNK, SSD_WIDTH), 1)
    pos_w = lane_w & (SSD_HEAD_DIM - 1)
    diag_w = row_w == pos_w
    causal_w = row_w >= pos_w
    lane_g = lax.broadcasted_iota(jnp.int32, (CHUNK, GROUP_WIDTH), 1)
    heads_per_group = SSD_HEADS // SSD_GROUPS

    rs = slice(c * CHUNK, (c + 1) * CHUNK)
    xs = xs_ref[rs, :]
    dt = dt_ref[rs, :]
    adt = dt * a_neg
    xdt = xs * dt
    a_cs = _dot(jnp.concatenate([tri] * 3, axis=1), jnp.concatenate(_split3(adt), axis=0))
    a_last = a_cs[CHUNK - 1:CHUNK, :]
    a_row = jnp.sum(jnp.where(diag_w, a_cs, 0.0), axis=0, keepdims=True)
    decay = jnp.exp2(jnp.where(causal_w, a_cs - a_row, -jnp.inf))
    decay_in = jnp.exp2(a_cs)
    decay_out = jnp.exp2(a_last - a_cs)
    chunk_decay = jnp.exp2(a_last)

    ys = []
    for g in range(SSD_GROUPS):
        gl = slice(g * GROUP_WIDTH, (g + 1) * GROUP_WIDTH)
        b_gb = bc_ref[rs, g * SSD_STATE:(g + 1) * SSD_STATE]
        c_gb = bc_ref[rs, (SSD_GROUPS + g) * SSD_STATE:(SSD_GROUPS + g + 1) * SSD_STATE]
        cb_rep = lax.dot_general(c_gb, jnp.concatenate([b_gb] * heads_per_group, axis=0), NT_DIMS,
                                 preferred_element_type=F32)
        m = (cb_rep * decay[:, gl]).astype(BF16)
        x_gb = xdt[:, gl].astype(BF16)
        x_blockdiag = jnp.concatenate(
            [jnp.where(lane_g >> HEAD_SHIFT == j, x_gb, jnp.zeros_like(x_gb)) for j in range(heads_per_group)],
            axis=0)
        y_diag = _dot(m, x_blockdiag)
        prev = state_scr[:, gl]
        y_off = _dot(c_gb, prev.astype(BF16)) * decay_in[:, gl]
        xd = (xdt[:, gl] * decay_out[:, gl]).astype(BF16)
        new_states = lax.dot_general(b_gb, xd, TN_DIMS, preferred_element_type=F32)
        state_scr[:, gl] = prev * chunk_decay[:, gl] + new_states
        ys.append(y_diag + y_off)
    y = (jnp.concatenate(ys, axis=1) + dskip * xs) * zs_ref[rs, :]
    ys_scr[rs, :] = (_rms_scale(y) * nw).astype(BF16)


def _ssd_kernel(xs_ref, bc_ref, dt_ref, zs_ref, alog_ref, dskip_ref, nw_ref, y_ref, state_scr):
    @pl.when(pl.program_id(1) == 0)
    def _():
        state_scr[...] = jnp.zeros_like(state_scr)

    a_neg = -jnp.exp(alog_ref[...]) * LOG2E
    dskip = dskip_ref[...]
    nw = nw_ref[...]
    for c in range(xs_ref.shape[0] // CHUNK):
        _ssd_chunk(c, xs_ref, bc_ref, dt_ref, zs_ref, a_neg, dskip, nw, y_ref, state_scr)


def _ssd_call(l, xs, bc, dt, zs, alog, dskip, nw, batch, seq):
    t = xs.shape[0]
    ts = SEQ_TILE
    nt = seq // ts
    tile = lambda w: pl.BlockSpec((ts, w), lambda b, i: (b * nt + i, 0))
    return pl.pallas_call(
        _ssd_kernel,
        grid=(batch, nt),
        in_specs=[tile(SSD_WIDTH), tile(XBC_DIM - SSD_WIDTH), tile(SSD_WIDTH), tile(SSD_WIDTH),
                  _layer(alog, l), _layer(dskip, l), _layer(nw, l)],
        out_specs=tile(SSD_WIDTH),
        out_shape=jax.ShapeDtypeStruct((t, SSD_WIDTH), BF16),
        scratch_shapes=[pltpu.VMEM((SSD_STATE, SSD_WIDTH), F32)],
        compiler_params=_params("arbitrary", "arbitrary"),
        name="ssd_scan",
    )(xs, bc, dt, zs, alog, dskip, nw)


def _attend(g, p, q_ref, ktp_ref, ktc_ref, vp_ref, vc_ref, bias_ref, ya_scr, first_tile):
    slab = ATTN_SLAB_HEADS * ATTN_HEAD_DIM
    lane_head = lax.broadcasted_iota(jnp.int32, (ATTN_GROUP_Q, slab), 1) >> HEAD_SHIFT
    lo = g * ATTN_GROUP_Q
    hi = lo + ATTN_GROUP_Q
    sl = slice(p * slab, (p + 1) * slab)
    cur = slice(max(lo - ATTN_HIST, 0), lo + ATTN_GROUP_K - ATTN_HIST)
    if lo < ATTN_HIST:
        kt = jnp.concatenate([ktp_ref[sl, lo:ATTN_HIST], ktc_ref[sl, cur]], axis=1)
        vs = jnp.concatenate([vp_ref[lo:ATTN_HIST, sl], vc_ref[cur, sl]], axis=0)
    else:
        kt, vs = ktc_ref[sl, cur], vc_ref[cur, sl]
    qg = q_ref[lo:hi, sl]
    zero = jnp.zeros_like(qg)
    qs = jnp.concatenate([jnp.where(lane_head == a, qg, zero) for a in range(ATTN_SLAB_HEADS)], axis=0)
    s = _dot(qs, kt) + bias_ref[p]
    if first_tile and lo < ATTN_HIST:
        band_pos = lax.broadcasted_iota(jnp.int32, (1, ATTN_GROUP_K), 1)
        s = jnp.where(band_pos >= ATTN_HIST - lo, s, jnp.finfo(F32).min)
    e = jnp.exp2(s - jnp.max(s, axis=-1, keepdims=True))
    denom = jnp.sum(e, axis=-1, keepdims=True)
    o = _dot(e.astype(BF16), vs) / denom
    og = o[:ATTN_GROUP_Q]
    for a in range(1, ATTN_SLAB_HEADS):
        og = jnp.where(lane_head == a, o[a * ATTN_GROUP_Q:(a + 1) * ATTN_GROUP_Q], og)
    ya_scr[lo:hi, sl] = og.astype(BF16)


def _attn_kernel(q_ref, ktp_ref, ktc_ref, vp_ref, vc_ref, bias_ref, ys_ref, x_ref, mod_ref, ws_ref, wa_ref,
                 o_ref, ya_scr):
    ts = q_ref.shape[0]
    slabs = ATTN_WIDTH // (ATTN_SLAB_HEADS * ATTN_HEAD_DIM)

    def attention(first_tile):
        for g in range(ts // ATTN_GROUP_Q):
            for p in range(slabs):
                _attend(g, p, q_ref, ktp_ref, ktc_ref, vp_ref, vc_ref, bias_ref, ya_scr, first_tile)

    first = pl.program_id(1) == 0
    pl.when(first)(lambda: attention(True))
    pl.when(jnp.logical_not(first))(lambda: attention(False))

    mix = _dot(ys_ref[...], ws_ref[...]) + _dot(ya_scr[...], wa_ref[...])
    o_ref[...] = x_ref[...] + mod_ref[0, 2:3, :] * mix


def _attn_call(l, q, kt, v, bias, ys, x, mod, wo, batch, seq):
    t, d = x.shape
    ts = SEQ_TILE
    nt = seq // ts
    per_tile = ts // ATTN_HIST
    hist = lambda b, i: jnp.maximum((b * nt + i) * per_tile - 1, 0)
    tile = lambda w: pl.BlockSpec((ts, w), lambda b, i: (b * nt + i, 0))
    prev = pl.BlockSpec((ATTN_HIST, ATTN_WIDTH), lambda b, i: (hist(b, i), 0))
    cur_t = pl.BlockSpec((ATTN_WIDTH, ts), lambda b, i: (0, b * nt + i))
    prev_t = pl.BlockSpec((ATTN_WIDTH, ATTN_HIST), lambda b, i: (0, hist(b, i)))
    return pl.pallas_call(
        _attn_kernel,
        grid=(batch, nt),
        in_specs=[tile(ATTN_WIDTH), prev_t, cur_t, prev, tile(ATTN_WIDTH), _layer(bias, l),
                  tile(SSD_WIDTH), tile(d), pl.BlockSpec((None, 1, 6, d), lambda b, i: (l, b, 0, 0)),
                  _layer(wo, l, (SSD_WIDTH, d), (0, 0)), _layer(wo, l, (ATTN_WIDTH, d), (1, 0))],
        out_specs=tile(d),
        out_shape=jax.ShapeDtypeStruct((t, d), F32),
        scratch_shapes=[pltpu.VMEM((ts, ATTN_WIDTH), BF16)],
        compiler_params=_params("arbitrary", "arbitrary"),
        name="attn_out",
    )(q, kt, kt, v, v, bias, ys, x, mod, wo, wo)


def _ffn_kernel(x_ref, mod_ref, nw_ref, wu_ref, cw_ref, wd_ref, fnw_ref, o_ref,
                h_scr, u_scr, act_scr, *, tiles_per_seq, final):
    tm = x_ref.shape[0]
    i = pl.program_id(0)
    x = x_ref[...]
    h = _rms_scale(x) * nw_ref[...]
    h = h * (1.0 + mod_ref[0, 4:5, :]) + mod_ref[0, 3:4, :]

    @pl.when(i % tiles_per_seq == 0)
    def _():
        h_scr[0:HALO, :] = jnp.zeros((HALO, D_MODEL), BF16)

    h_scr[HALO:HALO + tm, :] = h.astype(BF16)

    for c in range(FFN_DIM // FFN_CHUNK):
        gate = slice(c * FFN_CHUNK, (c + 1) * FFN_CHUNK)
        value = slice(FFN_DIM + c * FFN_CHUNK, FFN_DIM + (c + 1) * FFN_CHUNK)
        pick = lambda ref, rows: jnp.concatenate([ref[rows, gate], ref[rows, value]], axis=1)
        buf = u_scr.at[c % 2]
        buf[...] = _dot(h_scr[...], pick(wu_ref, slice(None)))
        u = buf[HALO:HALO + tm, :] * pick(cw_ref, slice(FFN_CONV - 1, FFN_CONV))
        for k in range(FFN_CONV - 1):
            off = HALO - (FFN_CONV - 1) + k
            u = u + buf[off:off + tm, :] * pick(cw_ref, slice(k, k + 1))
        act_scr[:, gate] = (_silu(u[:, :FFN_CHUNK]) * u[:, FFN_CHUNK:]).astype(BF16)

    h_scr[0:HALO, :] = h_scr[tm:tm + HALO, :]
    out = x + mod_ref[0, 5:6, :] * _dot(act_scr[...], wd_ref[...])
    if final:
        out = _rms_scale(out) * fnw_ref[...]
    o_ref[...] = out


def _ffn_call(l, x, mod, nw, wu, cw, wd, fnw, seq, final):
    t, d = x.shape
    tm = ROW_TILE
    tiles_per_seq = seq // tm
    row = pl.BlockSpec((tm, d), lambda i: (i, 0))
    return pl.pallas_call(
        functools.partial(_ffn_kernel, tiles_per_seq=tiles_per_seq, final=final),
        grid=(t // tm,),
        in_specs=[row, pl.BlockSpec((None, 1, 6, d), lambda i: (l, i // tiles_per_seq, 0, 0)), _layer(nw, l),
                  _layer(wu, l), _layer(cw, l), _layer(wd, l), _resident((1, d))],
        out_specs=row,
        out_shape=jax.ShapeDtypeStruct((t, d), F32),
        scratch_shapes=[
            pltpu.VMEM((tm + HALO, d), BF16),
            pltpu.VMEM((2, tm + HALO, 2 * FFN_CHUNK), F32),
            pltpu.VMEM((tm, FFN_DIM), BF16),
        ],
        compiler_params=_params("arbitrary"),
        name="conv_ffn",
    )(x, mod, nw, wu, cw, wd, fnw)


def _rel_bias_table(rel_bias):
    depth, heads = rel_bias.shape[:2]
    rb = rel_bias.astype(F32).reshape(depth * heads, -1) * LOG2E
    period = ATTN_GROUP_Q + ATTN_GROUP_K
    far = rb[:, 2 * MAX_REL:]
    flat_len = LEFT_CHUNKS * CHUNK - MAX_REL
    unseen = period - flat_len - 2 * MAX_REL
    w = jnp.concatenate([jnp.repeat(far, flat_len, axis=1), rb[:, :0:-1], jnp.repeat(far, unseen, axis=1)], axis=1)
    assert w.shape[1] == period
    toep = jnp.tile(w, (1, ATTN_GROUP_Q))[:, :ATTN_GROUP_Q * (period - 1)]
    toep = toep.reshape(depth * heads, ATTN_GROUP_Q, period - 1)[:, :, :ATTN_GROUP_K]
    first = (np.arange(ATTN_GROUP_Q)[:, None] // CHUNK) * CHUNK
    pos = np.arange(ATTN_GROUP_K)[None, :]
    window = (pos >= first) & (pos < first + BAND)
    table = jnp.where(window[None], toep, -1e30)
    return table.reshape(depth, heads // ATTN_SLAB_HEADS, ATTN_SLAB_HEADS * ATTN_GROUP_Q, ATTN_GROUP_K)


def _per_lane(v):
    return jnp.repeat(v.astype(F32), SSD_HEAD_DIM, axis=1)[:, None, :]


def kernel(x, c, norm_mix_w, w_ada, b_ada, w_in, ssd_conv_w, ssd_conv_b, dt_bias, a_log, d_skip, ssd_norm_w,
           rel_bias, w_out, norm_ffn_w, w_up, ffn_conv_w, w_down, final_norm_w):
    batch, seq, d = x.shape
    depth = w_ada.shape[0]
    assert d == D_MODEL and seq % SEQ_TILE == 0 and seq % ROW_TILE == 0
    mod = _ada_call(c, w_ada, b_ada).reshape(depth, batch, 6, d)

    o_xbc, o_dt, o_q = SSD_WIDTH, SSD_WIDTH + XBC_DIM, SSD_WIDTH + XBC_DIM + SSD_HEADS
    o_k, o_v = o_q + ATTN_WIDTH, o_q + 2 * ATTN_WIDTH
    cols = lambda lo, hi: w_in[:, :, lo:hi].astype(BF16)
    wz, wxbc, wq, wv = cols(0, o_xbc), cols(o_xbc, o_dt), cols(o_q, o_k), cols(o_v, None)
    wdt = jnp.repeat(cols(o_dt, o_q), SSD_HEAD_DIM, axis=2)
    wkt = jnp.swapaxes(cols(o_k, o_v), 1, 2)
    wo, wu, wd = w_out.astype(BF16), w_up.astype(BF16), w_down.astype(BF16)
    row3 = lambda a: a[:, None, :]
    nmw, nfw, cb, snw = row3(norm_mix_w), row3(norm_ffn_w), row3(ssd_conv_b), row3(ssd_norm_w)
    dtb, alog, dskip = _per_lane(dt_bias), _per_lane(a_log), _per_lane(d_skip)
    bias = _rel_bias_table(rel_bias)

    xf = x.reshape(batch * seq, d)
    for l in range(depth):
        zs, xs, bc, dt, q, kt, v = _inproj_call(l, xf, mod, nmw, wz, wxbc, wdt, wq, wkt, wv, ssd_conv_w, cb, dtb, seq)
        y_ssd = _ssd_call(l, xs, bc, dt, zs, alog, dskip, snw, batch, seq)
        xf = _attn_call(l, q, kt, v, bias, y_ssd, xf, mod, wo, batch, seq)
        xf = _ffn_call(l, xf, mod, nfw, wu, ffn_conv_w, wd, final_norm_w[None, :], seq, final=(l == depth - 1))
    return xf.reshape(batch, seq, d)
```
